```python
import math
import jax, jax.numpy as jnp
from jax import lax
import numpy as np

D_MODEL = 4096
BATCH = 2
SEQ = 4096
DEPTH = 4

N_MIXERS = 3
N_MLA = (DEPTH + 2) // 3
N_SB = (DEPTH + 1) // 3
N_DSA = DEPTH // 3

Q_BLOCK = 128
EPS = 1e-6

MLA_HEADS = D_MODEL // 128
MLA_Q_LORA = D_MODEL // 4
MLA_KV_LORA = 512
MLA_NOPE = 128
MLA_ROPE = 64
MLA_V = 128
ROPE_THETA = 10000.0

SB_HEADS = D_MODEL // 128
SB_HEAD_DIM = 128

DSA_HEADS = D_MODEL // 128
DSA_KV_HEADS = 8
DSA_HEAD_DIM = 128
IDX_HEADS = D_MODEL // 128
IDX_DIM = 128
DSA_TOPK_MAX = 256

FFN_DIM = 11008
CONV_WIDTH = 3

kernel_name = 'hybrid_mla_stickbreak_dsa_convffn'


def rms_norm(x, g):
    xf = x.astype(jnp.float32)
    y = xf * lax.rsqrt(jnp.mean(xf * xf, axis=-1, keepdims=True) + EPS)
    return (y * g.astype(jnp.float32)).astype(x.dtype)


def to_blocks(t):
    return jnp.moveaxis(t.reshape(t.shape[0], t.shape[1] // Q_BLOCK, Q_BLOCK, *t.shape[2:]), 1, 0)


def from_blocks(t):
    nb, b, qb = t.shape[:3]
    return jnp.moveaxis(t, 0, 1).reshape(b, nb * qb, *t.shape[3:])


def rope_cos_sin(positions, dim):
    inv = ROPE_THETA ** (-jnp.arange(0, dim, 2, dtype=jnp.float32) / dim)
    ang = positions.astype(jnp.float32)[..., None] * inv
    return jnp.cos(ang), jnp.sin(ang)


def apply_rope(t, cos, sin):
    t1, t2 = jnp.split(t.astype(jnp.float32), 2, axis=-1)
    return jnp.concatenate([t1 * cos - t2 * sin, t1 * sin + t2 * cos], axis=-1).astype(t.dtype)


def alibi_slopes(n_heads):
    return jnp.exp2(-8.0 * jnp.arange(1, n_heads + 1, dtype=jnp.float32) / n_heads)


def mla_mixer(h, positions, w_a, q_norm, kv_norm, w_uq, w_ukv, w_o):
    B, S, _ = h.shape
    cq, ckv, k_rope = jnp.split(h @ w_a, [MLA_Q_LORA, MLA_Q_LORA + MLA_KV_LORA], axis=-1)
    cq = rms_norm(cq, q_norm)
    ckv = rms_norm(ckv, kv_norm)
    q = (cq @ w_uq).reshape(B, S, MLA_HEADS, MLA_NOPE + MLA_ROPE)
    q_nope, q_rope = jnp.split(q, [MLA_NOPE], axis=-1)
    kv = (ckv @ w_ukv).reshape(B, S, MLA_HEADS, MLA_NOPE + MLA_V)
    k_nope, v = jnp.split(kv, [MLA_NOPE], axis=-1)
    cos, sin = rope_cos_sin(positions, MLA_ROPE)
    q_rope = apply_rope(q_rope, cos[:, :, None], sin[:, :, None])
    k_rope = apply_rope(k_rope, cos, sin)
    scale = (MLA_NOPE + MLA_ROPE) ** -0.5
    keys = jnp.arange(S)

    def block(args):
        qn, qr, i = args
        rows = i * Q_BLOCK + jnp.arange(Q_BLOCK)
        s = (jnp.einsum('bqhd,bkhd->bhqk', qn, k_nope)
             + jnp.einsum('bqhr,bkr->bhqk', qr, k_rope)).astype(jnp.float32) * scale
        s = jnp.where(keys[None, :] <= rows[:, None], s, -jnp.inf)
        p = jax.nn.softmax(s, axis=-1).astype(v.dtype)
        return jnp.einsum('bhqk,bkhd->bqhd', p, v)

    o = from_blocks(lax.map(block, (to_blocks(q_nope), to_blocks(q_rope), jnp.arange(S // Q_BLOCK))))
    return o.reshape(B, S, MLA_HEADS * MLA_V) @ w_o


def sb_mixer(h, w_qkv, w_o):
    B, S, _ = h.shape
    qkv = (h @ w_qkv).reshape(B, S, 3, SB_HEADS, SB_HEAD_DIM)
    q, k, v = qkv[:, :, 0], qkv[:, :, 1], qkv[:, :, 2]
    scale = SB_HEAD_DIM ** -0.5
    keys = jnp.arange(S)

    def block(args):
        qb, i = args
        rows = i * Q_BLOCK + jnp.arange(Q_BLOCK)
        z = jnp.einsum('bqhd,bkhd->bhqk', qb, k).astype(jnp.float32) * scale
        past = keys[None, :] < rows[:, None]
        log_keep = jnp.where(past, jax.nn.log_sigmoid(-z), 0.0)
        later = lax.cumsum(log_keep, axis=3, reverse=True) - log_keep
        a = jnp.where(past, jnp.exp(jax.nn.log_sigmoid(z) + later), 0.0).astype(v.dtype)
        return jnp.einsum('bhqk,bkhd->bqhd', a, v)

    o = from_blocks(lax.map(block, (to_blocks(q), jnp.arange(S // Q_BLOCK))))
    return o.reshape(B, S, SB_HEADS * SB_HEAD_DIM) @ w_o


def dsa_mixer(h, positions, w_in, idx_k_norm, w_o):
    B, S, _ = h.shape
    top_k = min(DSA_TOPK_MAX, S // 4)
    group = DSA_HEADS // DSA_KV_HEADS
    splits = [DSA_HEADS * DSA_HEAD_DIM]
    splits.append(splits[-1] + DSA_KV_HEADS * DSA_HEAD_DIM)
    splits.append(splits[-1] + DSA_KV_HEADS * DSA_HEAD_DIM)
    splits.append(splits[-1] + IDX_HEADS * IDX_DIM)
    splits.append(splits[-1] + IDX_DIM)
    q, k, v, iq, ik, iw = jnp.split(h @ w_in, splits, axis=-1)
    q = q.reshape(B, S, DSA_HEADS, DSA_HEAD_DIM)
    k = k.reshape(B, S, DSA_KV_HEADS, DSA_HEAD_DIM)
    v = v.reshape(B, S, DSA_KV_HEADS, DSA_HEAD_DIM)
    iq = iq.reshape(B, S, IDX_HEADS, IDX_DIM)
    ik = rms_norm(ik, idx_k_norm)
    iw = iw * IDX_HEADS ** -0.5
    slopes = alibi_slopes(DSA_HEADS).reshape(DSA_KV_HEADS, group)
    keys = jnp.arange(S)
    bidx = jnp.arange(B)[:, None, None]
    scale = DSA_HEAD_DIM ** -0.5

    def block(args):
        qb, iqb, iwb, pb, i = args
        rows = i * Q_BLOCK + jnp.arange(Q_BLOCK)
        dots = jnp.einsum('bqhd,bkd->bqhk', iqb, ik).astype(jnp.float32) * IDX_DIM ** -0.5
        score = jnp.einsum('bqh,bqhk->bqk', iwb.astype(jnp.float32), jax.nn.relu(dots))
        score = jnp.where((keys[None, :] <= rows[:, None])[None], score, -jnp.inf)
        _, idx = lax.top_k(score, top_k)
        valid = idx <= rows[None, :, None]
        k_sel = k[bidx, idx]
        v_sel = v[bidx, idx]
        pos_sel = positions[bidx, idx]
        qg = qb.reshape(B, Q_BLOCK, DSA_KV_HEADS, group, DSA_HEAD_DIM)
        s = jnp.einsum('bqgrd,bqkgd->bqgrk', qg, k_sel).astype(jnp.float32) * scale
        dist = jnp.abs(pb[:, :, None] - pos_sel).astype(jnp.float32)
        s = s - slopes[None, None, :, :, None] * dist[:, :, None, None, :]
        s = jnp.where(valid[:, :, None, None, :], s, -jnp.inf)
        p = jax.nn.softmax(s, axis=-1).astype(v.dtype)
        o = jnp.einsum('bqgrk,bqkgd->bqgrd', p, v_sel)
        return o.reshape(B, Q_BLOCK, DSA_HEADS, DSA_HEAD_DIM)

    o = from_blocks(lax.map(block, (to_blocks(q), to_blocks(iq), to_blocks(iw),
                                    to_blocks(positions), jnp.arange(S // Q_BLOCK))))
    return o.reshape(B, S, DSA_HEADS * DSA_HEAD_DIM) @ w_o


def conv_ffn(h, w_up, conv_w, conv_b, w_down):
    S = h.shape[1]
    u = h @ w_up
    up = jnp.pad(u, ((0, 0), (CONV_WIDTH - 1, 0), (0, 0)))
    c = conv_b
    for j in range(CONV_WIDTH):
        c = c + conv_w[j] * up[:, j:j + S]
    gate, val = jnp.split(c, 2, axis=-1)
    return (jax.nn.silu(gate) * val) @ w_down


def setup_inputs(seed: int = 0) -> dict:
    key = jax.random.key(seed)
    ks = jax.random.split(key, 24)
    out_scale = (2 * DEPTH) ** -0.5

    def w(k, shape, fan_in, scale=1.0):
        return jax.random.normal(k, shape, jnp.float32) * (scale * fan_in ** -0.5)

    def gain(k, shape):
        return 1.0 + 0.02 * jax.random.normal(k, shape, jnp.float32)

    x = jax.random.normal(ks[0], (BATCH, SEQ, D_MODEL), jnp.float32)
    positions = jnp.broadcast_to(jnp.arange(SEQ, dtype=jnp.int32), (BATCH, SEQ))
    dsa_cols = (DSA_HEADS + 2 * DSA_KV_HEADS) * DSA_HEAD_DIM + IDX_HEADS * IDX_DIM + IDX_DIM + IDX_HEADS
    return {
        'x': x,
        'positions': positions,
        'attn_norm': gain(ks[1], (DEPTH, D_MODEL)),
        'ffn_norm': gain(ks[2], (DEPTH, D_MODEL)),
        'final_norm': gain(ks[3], (D_MODEL,)),
        'ffn_w_up': w(ks[4], (DEPTH, D_MODEL, 2 * FFN_DIM), D_MODEL),
        'ffn_conv_w': w(ks[5], (DEPTH, CONV_WIDTH, 2 * FFN_DIM), CONV_WIDTH),
        'ffn_conv_b': 0.01 * jax.random.normal(ks[6], (DEPTH, 2 * FFN_DIM), jnp.float32),
        'ffn_w_down': w(ks[7], (DEPTH, FFN_DIM, D_MODEL), FFN_DIM, out_scale),
        'mla_w_a': w(ks[8], (N_MLA, D_MODEL, MLA_Q_LORA + MLA_KV_LORA + MLA_ROPE), D_MODEL),
        'mla_q_norm': gain(ks[9], (N_MLA, MLA_Q_LORA)),
        'mla_kv_norm': gain(ks[10], (N_MLA, MLA_KV_LORA)),
        'mla_w_uq': w(ks[11], (N_MLA, MLA_Q_LORA, MLA_HEADS * (MLA_NOPE + MLA_ROPE)), MLA_Q_LORA),
        'mla_w_ukv': w(ks[12], (N_MLA, MLA_KV_LORA, MLA_HEADS * (MLA_NOPE + MLA_V)), MLA_KV_LORA),
        'mla_w_o': w(ks[13], (N_MLA, MLA_HEADS * MLA_V, D_MODEL), MLA_HEADS * MLA_V, out_scale),
        'sb_w_qkv': w(ks[14], (N_SB, D_MODEL, 3 * SB_HEADS * SB_HEAD_DIM), D_MODEL),
        'sb_w_o': w(ks[15], (N_SB, SB_HEADS * SB_HEAD_DIM, D_MODEL), SB_HEADS * SB_HEAD_DIM, out_scale),
        'dsa_w_in': w(ks[16], (N_DSA, D_MODEL, dsa_cols), D_MODEL),
        'dsa_idx_k_norm': gain(ks[17], (N_DSA, IDX_DIM)),
        'dsa_w_o': w(ks[18], (N_DSA, DSA_HEADS * DSA_HEAD_DIM, D_MODEL), DSA_HEADS * DSA_HEAD_DIM, out_scale),
    }


def reference(x, positions, attn_norm, ffn_norm, final_norm, ffn_w_up, ffn_conv_w, ffn_conv_b,
              ffn_w_down, mla_w_a, mla_q_norm, mla_kv_norm, mla_w_uq, mla_w_ukv, mla_w_o,
              sb_w_qkv, sb_w_o, dsa_w_in, dsa_idx_k_norm, dsa_w_o):
    h = x
    for i in range(DEPTH):
        kind, j = i % N_MIXERS, i // N_MIXERS
        n = rms_norm(h, attn_norm[i])
        if kind == 0:
            mix = mla_mixer(n, positions, mla_w_a[j], mla_q_norm[j], mla_kv_norm[j],
                            mla_w_uq[j], mla_w_ukv[j], mla_w_o[j])
        elif kind == 1:
            mix = sb_mixer(n, sb_w_qkv[j], sb_w_o[j])
        else:
            mix = dsa_mixer(n, positions, dsa_w_in[j], dsa_idx_k_norm[j], dsa_w_o[j])
        h = h + mix
        h = h + conv_ffn(rms_norm(h, ffn_norm[i]), ffn_w_up[i], ffn_conv_w[i], ffn_conv_b[i], ffn_w_down[i])
    return rms_norm(h, final_norm)
```

```python
import functools

import jax
import jax.numpy as jnp
from jax import lax
from jax.experimental import pallas as pl
from jax.experimental.pallas import tpu as pltpu

F32 = jnp.float32
BF16 = jnp.bfloat16

DEPTH = 4
N_MIXERS = 3
EPS = 1e-6
HEAD = 128
N_HEADS = 32
MLA_Q_LORA = 1024
MLA_KV_LORA = 512
MLA_NOPE = 128
MLA_ROPE = 64
ROPE_THETA = 10000.0
DSA_KV_HEADS = 8
DSA_GROUP = N_HEADS // DSA_KV_HEADS
DSA_TOPK_MAX = 256
FFN_DIM = 11008
CONV_WIDTH = 3

LANES = 128
SUBLANES = 8
VMEM_CAP = 56 * 1024 * 1024
NEG_BIG = -1e30
INT_MIN = -2 ** 31


def _params(sem, vmem_bytes):
    return pltpu.CompilerParams(dimension_semantics=sem,
                                vmem_limit_bytes=int(min(VMEM_CAP, max(vmem_bytes, 16 * 1024 * 1024))))


def _dot_nt(a, b):
    return lax.dot_general(a, b, (((1,), (1,)), ((), ())), preferred_element_type=F32)


def _rmsnorm_kernel(x_ref, g_ref, o_ref):
    x = x_ref[...]
    y = x * lax.rsqrt(jnp.mean(x * x, axis=-1, keepdims=True) + EPS)
    o_ref[...] = (y * g_ref[...]).astype(o_ref.dtype)


def rmsnorm(x, g, out_dtype, tm=256):
    t, d = x.shape
    return pl.pallas_call(
        _rmsnorm_kernel,
        grid=(t // tm,),
        in_specs=[pl.BlockSpec((tm, d), lambda i: (i, 0)), pl.BlockSpec((1, d), lambda i: (0, 0))],
        out_specs=pl.BlockSpec((tm, d), lambda i: (i, 0)),
        out_shape=jax.ShapeDtypeStruct((t, d), out_dtype),
        compiler_params=_params(("parallel",), 6 * tm * d * 4),
        name="rmsnorm",
    )(x, g.reshape(1, d))


def _proj_kernel(*refs, has_res, heads_out):
    if has_res:
        x_ref, w_ref, r_ref, o_ref, wb_ref = refs
    else:
        x_ref, w_ref, o_ref, wb_ref = refs

    @pl.when(pl.program_id(1) == 0)
    def _():
        wb_ref[...] = w_ref[...].astype(BF16)

    acc = jnp.dot(x_ref[...], wb_ref[...], preferred_element_type=F32)
    if has_res:
        acc = r_ref[...] + acc
    if heads_out:
        for c in range(acc.shape[1] // HEAD):
            o_ref[0, c] = acc[:, c * HEAD:(c + 1) * HEAD].astype(o_ref.dtype)
    else:
        o_ref[...] = acc.astype(o_ref.dtype)


def proj(x, w, n_cols, *, col_off=0, tm, tn, out_dtype, res=None, heads_batch=None):
    t, k = x.shape
    assert w.shape[0] == k and n_cols % tn == 0 and t % tm == 0 and col_off % tn == 0
    off_b = col_off // tn
    grid = (n_cols // tn, t // tm)
    in_specs = [pl.BlockSpec((tm, k), lambda j, i: (i, 0)),
                pl.BlockSpec((k, tn), lambda j, i: (0, j + off_b))]
    args = [x, w]
    if res is not None:
        in_specs.append(pl.BlockSpec((tm, tn), lambda j, i: (i, j)))
        args.append(res)
    if heads_batch is None:
        out_shape = jax.ShapeDtypeStruct((t, n_cols), out_dtype)
        out_spec = pl.BlockSpec((tm, tn), lambda j, i: (i, j))
    else:
        s = t // heads_batch
        assert s % tm == 0
        nsb = s // tm
        out_shape = jax.ShapeDtypeStruct((heads_batch, n_cols // HEAD, s, HEAD), out_dtype)
        out_spec = pl.BlockSpec((1, tn // HEAD, tm, HEAD), lambda j, i: (i // nsb, j, i % nsb, 0))
    vmem = 2 * k * tn * 4 + k * tn * 2 + 2 * tm * k * 2 + 6 * tm * tn * 4
    return pl.pallas_call(
        functools.partial(_proj_kernel, has_res=res is not None, heads_out=heads_batch is not None),
        grid=grid,
        in_specs=in_specs,
        out_specs=out_spec,
        out_shape=out_shape,
        scratch_shapes=[pltpu.VMEM((k, tn), BF16)],
        compiler_params=_params(("arbitrary", "arbitrary"), vmem + (4 << 20)),
        name="proj",
    )(*args)


def _down_kernel(x_ref, w_ref, r_ref, o_ref):
    o_ref[...] = r_ref[...] + jnp.dot(x_ref[...], w_ref[...], preferred_element_type=F32)


def down_proj(x, w_bf16, res, *, tm=512, tn=256):
    t, k = x.shape
    n = w_bf16.shape[1]
    vmem = 2 * tm * k * 2 + 2 * k * tn * 2 + 6 * tm * tn * 4
    return pl.pallas_call(
        _down_kernel,
        grid=(t // tm, n // tn),
        in_specs=[pl.BlockSpec((tm, k), lambda i, j: (i, 0)),
                  pl.BlockSpec((k, tn), lambda i, j: (0, j)),
                  pl.BlockSpec((tm, tn), lambda i, j: (i, j))],
        out_specs=pl.BlockSpec((tm, tn), lambda i, j: (i, j)),
        out_shape=jax.ShapeDtypeStruct((t, n), F32),
        compiler_params=_params(("parallel", "parallel"), vmem + (4 << 20)),
        name="down_proj",
    )(x, w_bf16, res)


def _shift_rows(u, prev, k):
    rolled = pltpu.roll(u, k, axis=0)
    top = jnp.where(lax.broadcasted_iota(jnp.int32, (SUBLANES, u.shape[1]), 0) < k,
                    pltpu.roll(prev, k, axis=0), rolled[:SUBLANES])
    return jnp.concatenate([top, rolled[SUBLANES:]], axis=0)


def _ffn_up_kernel(x_ref, wg_ref, wv_ref, cwg_ref, cwv_ref, cbg_ref, cbv_ref, o_ref,
                   wgb_ref, wvb_ref, pg_ref, pv_ref, *, tiles_per_seq):
    i = pl.program_id(1)

    @pl.when(i == 0)
    def _():
        wgb_ref[...] = wg_ref[...].astype(BF16)
        wvb_ref[...] = wv_ref[...].astype(BF16)

    @pl.when(i % tiles_per_seq == 0)
    def _():
        pg_ref[...] = jnp.zeros_like(pg_ref)
        pv_ref[...] = jnp.zeros_like(pv_ref)

    x = x_ref[...]

    def conv(u, prev_ref, cw_ref, cb_ref):
        prev = prev_ref[...]
        tm = u.shape[0]
        c = cb_ref[...] + cw_ref[0:1, :] * _shift_rows(u, prev, 2)
        c = c + cw_ref[1:2, :] * _shift_rows(u, prev, 1)
        c = c + cw_ref[2:3, :] * u
        prev_ref[...] = u[tm - SUBLANES:]
        return c

    gate = conv(jnp.dot(x, wgb_ref[...], preferred_element_type=F32), pg_ref, cwg_ref, cbg_ref)
    val = conv(jnp.dot(x, wvb_ref[...], preferred_element_type=F32), pv_ref, cwv_ref, cbv_ref)
    o_ref[...] = (gate * jax.nn.sigmoid(gate) * val).astype(o_ref.dtype)


def ffn_up(x, w_up, conv_w, conv_b, seq_len, *, tm=512, tn=256):
    t, k = x.shape
    f = w_up.shape[1] // 2
    nfb = f // tn
    assert f % tn == 0 and seq_len % tm == 0 and t % seq_len == 0
    cb = conv_b.reshape(1, 2 * f)
    vmem = 4 * k * tn * 4 + 2 * k * tn * 2 + 2 * tm * k * 2 + 12 * tm * tn * 4
    return pl.pallas_call(
        functools.partial(_ffn_up_kernel, tiles_per_seq=seq_len // tm),
        grid=(nfb, t // tm),
        in_specs=[pl.BlockSpec((tm, k), lambda j, i: (i, 0)),
                  pl.BlockSpec((k, tn), lambda j, i: (0, j)),
                  pl.BlockSpec((k, tn), lambda j, i: (0, j + nfb)),
                  pl.BlockSpec((CONV_WIDTH, tn), lambda j, i: (0, j)),
                  pl.BlockSpec((CONV_WIDTH, tn), lambda j, i: (0, j + nfb)),
                  pl.BlockSpec((1, tn), lambda j, i: (0, j)),
                  pl.BlockSpec((1, tn), lambda j, i: (0, j + nfb))],
        out_specs=pl.BlockSpec((tm, tn), lambda j, i: (i, j)),
        out_shape=jax.ShapeDtypeStruct((t, f), BF16),
        scratch_shapes=[pltpu.VMEM((k, tn), BF16), pltpu.VMEM((k, tn), BF16),
                        pltpu.VMEM((SUBLANES, tn), F32), pltpu.VMEM((SUBLANES, tn), F32)],
        compiler_params=_params(("arbitrary", "arbitrary"), vmem + (4 << 20)),
        name="ffn_up",
    )(x, w_up, w_up, conv_w, conv_w, cb, cb)


def _mla_prep_kernel(a_ref, cs_ref, gq_ref, gkv_ref, cq_ref, ckv_ref, kr_ref):
    def norm(v, g):
        return v * lax.rsqrt(jnp.mean(v * v, axis=-1, keepdims=True) + EPS) * g

    cq_ref[...] = norm(a_ref[:, 0:MLA_Q_LORA], gq_ref[...]).astype(cq_ref.dtype)
    ckv_ref[...] = norm(a_ref[:, MLA_Q_LORA:MLA_Q_LORA + MLA_KV_LORA], gkv_ref[...]).astype(ckv_ref.dtype)
    off = MLA_Q_LORA + MLA_KV_LORA
    y = a_ref[:, off:off + LANES] * cs_ref[...]
    kr_ref[...] = (y + pltpu.roll(y, MLA_ROPE, axis=1)).astype(kr_ref.dtype)


def mla_prep(a, cs, gq, gkv, tm=512):
    t, na = a.shape
    return pl.pallas_call(
        _mla_prep_kernel,
        grid=(t // tm,),
        in_specs=[pl.BlockSpec((tm, na), lambda i: (i, 0)),
                  pl.BlockSpec((tm, LANES), lambda i: (i, 0)),
                  pl.BlockSpec((1, MLA_Q_LORA), lambda i: (0, 0)),
                  pl.BlockSpec((1, MLA_KV_LORA), lambda i: (0, 0))],
        out_specs=[pl.BlockSpec((tm, MLA_Q_LORA), lambda i: (i, 0)),
                   pl.BlockSpec((tm, MLA_KV_LORA), lambda i: (i, 0)),
                   pl.BlockSpec((tm, LANES), lambda i: (i, 0))],
        out_shape=[jax.ShapeDtypeStruct((t, MLA_Q_LORA), BF16),
                   jax.ShapeDtypeStruct((t, MLA_KV_LORA), BF16),
                   jax.ShapeDtypeStruct((t, LANES), BF16)],
        compiler_params=_params(("parallel",), 6 * tm * na * 4),
        name="mla_prep",
    )(a, cs, gq.reshape(1, -1), gkv.reshape(1, -1))


def _softmax_step(s, m, l, acc, v):
    m_new = jnp.maximum(m, jnp.max(s, axis=-1, keepdims=True))
    alpha = jnp.exp(m - m_new)
    p = jnp.exp(s - m_new)
    l = alpha * l + jnp.sum(p, axis=-1, keepdims=True)
    acc = alpha * acc + jnp.dot(p.astype(BF16), v, preferred_element_type=F32)
    return m_new, l, acc


def _mla_attn_kernel(qn_ref, qr_ref, kn_ref, v_ref, kr_ref, cs_ref, o_ref, qcat_ref, kcat_ref,
                     *, seq, tq, tk, scale):
    qcat_ref[:, 0:HEAD] = qn_ref[0, 0]
    qcat_ref[:, HEAD:2 * HEAD] = (qr_ref[0, 0].astype(F32) * cs_ref[0]).astype(BF16)
    kcat_ref[:, 0:HEAD] = kn_ref[0, 0]
    kcat_ref[:, HEAD:2 * HEAD] = kr_ref[0]

    def q_block(i, _):
        r0 = pl.multiple_of(i * tq, tq)
        q = qcat_ref[pl.ds(r0, tq), :]

        def step(c, carry, masked):
            m, l, acc = carry
            k0 = pl.multiple_of(c * tk, tk)
            s = _dot_nt(q, kcat_ref[pl.ds(k0, tk), :]) * scale
            if masked:
                rows = r0 + lax.broadcasted_iota(jnp.int32, (tq, tk), 0)
                cols = k0 + lax.broadcasted_iota(jnp.int32, (tq, tk), 1)
                s = jnp.where(cols <= rows, s, -jnp.inf)
            return _softmax_step(s, m, l, acc, v_ref[0, 0, pl.ds(k0, tk), :])

        n_full = (i * tq) // tk
        init = (jnp.full((tq, 1), -jnp.inf, F32), jnp.zeros((tq, 1), F32), jnp.zeros((tq, HEAD), F32))
        carry = lax.fori_loop(0, n_full, lambda c, cr: step(c, cr, False), init)
        _, l, acc = step(n_full, carry, True)
        o_ref[0, pl.ds(r0, tq), :] = (acc / l).astype(o_ref.dtype)
        return 0

    lax.fori_loop(0, seq // tq, q_block, 0)


def mla_attention(q_hm, kv_hm, kr, cs, *, tq=256, tk=512):
    b, _, s, _ = q_hm.shape
    blk = (1, 1, s, HEAD)
    return pl.pallas_call(
        functools.partial(_mla_attn_kernel, seq=s, tq=tq, tk=tk, scale=(MLA_NOPE + MLA_ROPE) ** -0.5),
        grid=(b, N_HEADS),
        in_specs=[pl.BlockSpec(blk, lambda bi, h: (bi, h, 0, 0)),
                  pl.BlockSpec(blk, lambda bi, h: (bi, N_HEADS + h, 0, 0)),
                  pl.BlockSpec(blk, lambda bi, h: (bi, 2 * h, 0, 0)),
                  pl.BlockSpec(blk, lambda bi, h: (bi, 2 * h + 1, 0, 0)),
                  pl.BlockSpec((1, s, HEAD), lambda bi, h: (bi, 0, 0)),
                  pl.BlockSpec((1, s, HEAD), lambda bi, h: (bi, 0, 0))],
        out_specs=pl.BlockSpec((1, s, HEAD), lambda bi, h: (bi, 0, h)),
        out_shape=jax.ShapeDtypeStruct((b, s, N_HEADS * HEAD), BF16),
        scratch_shapes=[pltpu.VMEM((s, 2 * HEAD), BF16), pltpu.VMEM((s, 2 * HEAD), BF16)],
        compiler_params=_params(("parallel", "parallel"), 32 << 20),
        name="mla_attention",
    )(q_hm, q_hm, kv_hm, kv_hm, kr, cs)


def _sb_attn_kernel(q_ref, k_ref, v_ref, u_ref, o_ref, *, seq, t, scale):
    rows = lax.broadcasted_iota(jnp.int32, (t, t), 0)
    cols = lax.broadcasted_iota(jnp.int32, (t, t), 1)
    past = cols < rows

    def q_block(i, _):
        r0 = pl.multiple_of(i * t, t)
        q = q_ref[0, 0, pl.ds(r0, t), :]

        def step(j, carry, diag):
            c, acc = carry
            k0 = pl.multiple_of(j * t, t)
            z = _dot_nt(q, k_ref[0, 0, pl.ds(k0, t), :]) * scale
            log_beta = jnp.minimum(z, 0.0) - jnp.log1p(jnp.exp(-jnp.abs(z)))
            log_keep = log_beta - z
            if diag:
                log_keep = jnp.where(past, log_keep, 0.0)
            hi = log_keep.astype(BF16)
            lo = (log_keep - hi.astype(F32)).astype(BF16)
            tri = u_ref[...]
            later = (jnp.dot(hi, tri, preferred_element_type=F32)
                     + jnp.dot(lo, tri, preferred_element_type=F32))
            a = jnp.exp(log_beta + later + c)
            if diag:
                a = jnp.where(past, a, 0.0)
            acc = acc + jnp.dot(a.astype(BF16), v_ref[0, 0, pl.ds(k0, t), :], preferred_element_type=F32)
            c = c + jnp.sum(log_keep, axis=-1, keepdims=True)
            return c, acc

        carry = step(i, (jnp.zeros((t, 1), F32), jnp.zeros((t, HEAD), F32)), True)
        _, acc = lax.fori_loop(0, i, lambda n, cr: step(i - 1 - n, cr, False), carry)
        o_ref[0, pl.ds(r0, t), :] = acc.astype(o_ref.dtype)
        return 0

    lax.fori_loop(0, seq // t, q_block, 0)


def sb_attention(qkv_hm, *, t=256):
    b, _, s, _ = qkv_hm.shape
    blk = (1, 1, s, HEAD)
    tri = (jnp.arange(t)[:, None] > jnp.arange(t)[None, :]).astype(BF16)
    return pl.pallas_call(
        functools.partial(_sb_attn_kernel, seq=s, t=t, scale=HEAD ** -0.5),
        grid=(b, N_HEADS),
        in_specs=[pl.BlockSpec(blk, lambda bi, h: (bi, h, 0, 0)),
                  pl.BlockSpec(blk, lambda bi, h: (bi, N_HEADS + h, 0, 0)),
                  pl.BlockSpec(blk, lambda bi, h: (bi, 2 * N_HEADS + h, 0, 0)),
                  pl.BlockSpec((t, t), lambda bi, h: (0, 0))],
        out_specs=pl.BlockSpec((1, s, HEAD), lambda bi, h: (bi, 0, h)),
        out_shape=jax.ShapeDtypeStruct((b, s, N_HEADS * HEAD), BF16),
        compiler_params=_params(("parallel", "parallel"), 32 << 20),
        name="sb_attention",
    )(qkv_hm, qkv_hm, qkv_hm, tri)


DSA_TQ = 128
DSA_TK = 512


def _dsa_index_kernel(iq_ref, tail_ref, iw_ref, g_ref, bias_ref, ikn_ref, key_ref,
                      *, n_chunks, top_k, w_scale, heads_per_dot):
    i = pl.program_id(1)

    @pl.when(i == 0)
    def _():
        ik = tail_ref[0]
        y = ik * lax.rsqrt(jnp.mean(ik * ik, axis=-1, keepdims=True) + EPS)
        ikn_ref[...] = (y * g_ref[...]).astype(BF16)

    n_live = (i * DSA_TQ) // DSA_TK + 1
    w = iw_ref[0][:, 0:N_HEADS] * w_scale
    rows = i * DSA_TQ + lax.broadcasted_iota(jnp.int32, (DSA_TQ, DSA_TK), 0)
    cols0 = lax.broadcasted_iota(jnp.int32, (DSA_TQ, DSA_TK), 1)

    def score_chunk(c, _):
        k0 = pl.multiple_of(c * DSA_TK, DSA_TK)
        ik = ikn_ref[pl.ds(k0, DSA_TK), :]
        sc = jnp.zeros((DSA_TQ, DSA_TK), F32)
        for hg in range(N_HEADS // heads_per_dot):
            q = iq_ref[0, hg * heads_per_dot:(hg + 1) * heads_per_dot].reshape(heads_per_dot * DSA_TQ, HEAD)
            d = _dot_nt(q, ik)
            for hh in range(heads_per_dot):
                h = hg * heads_per_dot + hh
                sc = sc + w[:, h:h + 1] * jnp.maximum(d[hh * DSA_TQ:(hh + 1) * DSA_TQ], 0.0)
        bits = pltpu.bitcast(sc, jnp.int32)
        key = jnp.where(bits < 0, bits ^ jnp.int32(0x7FFFFFFF), bits)
        key_ref[c] = jnp.where(k0 + cols0 <= rows, key, jnp.int32(INT_MIN))
        return 0

    lax.fori_loop(0, n_live, score_chunk, 0)

    def count_ge(cand):
        def body(c, cnt):
            return cnt + jnp.sum((key_ref[c] >= cand).astype(jnp.int32), axis=-1, keepdims=True)
        return lax.fori_loop(0, n_live, body, jnp.zeros((DSA_TQ, 1), jnp.int32))

    def bit_step(n, thr):
        cand = thr + lax.shift_left(jnp.int32(1), jnp.int32(31) - n)
        return jnp.where(count_ge(cand) >= top_k, cand, thr)

    thr = lax.fori_loop(0, 32, bit_step, jnp.full((DSA_TQ, 1), INT_MIN, jnp.int32))
    thr = jnp.maximum(thr, jnp.int32(INT_MIN + 1))

    def write_live(c, _):
        bias_ref[0, 0, c] = jnp.where(key_ref[c] >= thr, 0.0, NEG_BIG).astype(bias_ref.dtype)
        return 0

    def write_dead(c, _):
        bias_ref[0, 0, c] = jnp.full((DSA_TQ, DSA_TK), NEG_BIG, bias_ref.dtype)
        return 0

    lax.fori_loop(0, n_live, write_live, 0)
    lax.fori_loop(n_live, n_chunks, write_dead, 0)


def dsa_index(iq_hm, tail, g_ik, top_k):
    b, _, s, _ = iq_hm.shape
    nq, nc = s // DSA_TQ, s // DSA_TK
    w_scale = N_HEADS ** -0.5 * HEAD ** -0.5
    return pl.pallas_call(
        functools.partial(_dsa_index_kernel, n_chunks=nc, top_k=top_k, w_scale=w_scale, heads_per_dot=8),
        grid=(b, nq),
        in_specs=[pl.BlockSpec((1, N_HEADS, DSA_TQ, HEAD), lambda bi, i: (bi, 0, i, 0)),
                  pl.BlockSpec((1, s, HEAD), lambda bi, i: (bi, 0, 0)),
                  pl.BlockSpec((1, DSA_TQ, HEAD), lambda bi, i: (bi, i, 1)),
                  pl.BlockSpec((1, HEAD), lambda bi, i: (0, 0))],
        out_specs=pl.BlockSpec((1, 1, nc, DSA_TQ, DSA_TK), lambda bi, i: (bi, i, 0, 0, 0)),
        out_shape=jax.ShapeDtypeStruct((b, nq, nc, DSA_TQ, DSA_TK), BF16),
        scratch_shapes=[pltpu.VMEM((s, HEAD), BF16), pltpu.VMEM((nc, DSA_TQ, DSA_TK), jnp.int32)],
        compiler_params=_params(("arbitrary", "arbitrary"), 40 << 20),
        name="dsa_index",
    )(iq_hm, tail, tail, g_ik.reshape(1, HEAD))


def _dsa_attn_kernel(q_ref, k_ref, v_ref, bias_ref, pq_ref, pk_ref, slope_ref, o_ref, *, scale):
    i = pl.program_id(2)
    rq = DSA_GROUP * DSA_TQ
    q = q_ref[0].reshape(rq, HEAD)
    slope = slope_ref[0].reshape(DSA_GROUP, DSA_TQ, 1)
    pq = pq_ref[0]

    def step(c, carry):
        m, l, acc = carry
        k0 = pl.multiple_of(c * DSA_TK, DSA_TK)
        s = _dot_nt(q, k_ref[0, 0, pl.ds(k0, DSA_TK), :]) * scale
        dist = jnp.abs(pq - pk_ref[0, c]).astype(F32)
        pen = bias_ref[0, 0, c].astype(F32)
        s = s.reshape(DSA_GROUP, DSA_TQ, DSA_TK) - slope * dist[None] + pen[None]
        return _softmax_step(s.reshape(rq, DSA_TK), m, l, acc, v_ref[0, 0, pl.ds(k0, DSA_TK), :])

    init = (jnp.full((rq, 1), -jnp.inf, F32), jnp.zeros((rq, 1), F32), jnp.zeros((rq, HEAD), F32))
    _, l, acc = lax.fori_loop(0, (i * DSA_TQ) // DSA_TK + 1, step, init)
    o = acc / l
    for r in range(DSA_GROUP):
        o_ref[0, :, r * HEAD:(r + 1) * HEAD] = o[r * DSA_TQ:(r + 1) * DSA_TQ].astype(o_ref.dtype)


def dsa_attention(qkv_hm, bias, positions, slopes):
    b, _, s, _ = qkv_hm.shape
    nq, nc = s // DSA_TQ, s // DSA_TK
    kv = (1, 1, s, HEAD)
    slope_rows = jnp.repeat(slopes.reshape(DSA_KV_HEADS, DSA_GROUP), DSA_TQ, axis=1)[..., None]
    return pl.pallas_call(
        functools.partial(_dsa_attn_kernel, scale=HEAD ** -0.5),
        grid=(b, DSA_KV_HEADS, nq),
        in_specs=[pl.BlockSpec((1, DSA_GROUP, DSA_TQ, HEAD), lambda bi, g, i: (bi, g, i, 0)),
                  pl.BlockSpec(kv, lambda bi, g, i: (bi, N_HEADS + g, 0, 0)),
                  pl.BlockSpec(kv, lambda bi, g, i: (bi, N_HEADS + DSA_KV_HEADS + g, 0, 0)),
                  pl.BlockSpec((1, 1, nc, DSA_TQ, DSA_TK), lambda bi, g, i: (bi, i, 0, 0, 0)),
                  pl.BlockSpec((1, DSA_TQ, 1), lambda bi, g, i: (bi, i, 0)),
                  pl.BlockSpec((1, nc, 1, DSA_TK), lambda bi, g, i: (bi, 0, 0, 0)),
                  pl.BlockSpec((1, DSA_GROUP * DSA_TQ, 1), lambda bi, g, i: (g, 0, 0))],
        out_specs=pl.BlockSpec((1, DSA_TQ, DSA_GROUP * HEAD), lambda bi, g, i: (bi, i, g)),
        out_shape=jax.ShapeDtypeStruct((b, s, N_HEADS * HEAD), BF16),
        compiler_params=_params(("parallel", "parallel", "arbitrary"), 32 << 20),
        name="dsa_attention",
    )(qkv_hm, qkv_hm, qkv_hm, bias, positions.reshape(b, s, 1), positions.reshape(b, nc, 1, DSA_TK),
      slope_rows)


def _rotate_half_cols(r):
    half = r.shape[-1] // 2
    return jnp.concatenate([-r[..., half:], r[..., :half]], axis=-1)


def _mla_layer(n, h, cs, b, w_a, q_norm, kv_norm, w_uq, w_ukv, w_o):
    t, d = n.shape
    lat = MLA_Q_LORA + MLA_KV_LORA
    kr_w = w_a[:, lat:lat + MLA_ROPE]
    w_a2 = jnp.concatenate([w_a[:, :lat], kr_w, _rotate_half_cols(kr_w), jnp.zeros((d, LANES), F32)], axis=1)
    a = proj(n, w_a2, w_a2.shape[1], tm=512, tn=256, out_dtype=F32)
    cq, ckv, kr = mla_prep(a, cs, q_norm, kv_norm)
    w3 = w_uq.reshape(MLA_Q_LORA, N_HEADS, MLA_NOPE + MLA_ROPE)
    rope_w = w3[:, :, MLA_NOPE:]
    w_uq2 = jnp.concatenate([w3[:, :, :MLA_NOPE].reshape(MLA_Q_LORA, -1),
                             jnp.concatenate([rope_w, _rotate_half_cols(rope_w)], -1).reshape(MLA_Q_LORA, -1)], 1)
    q_hm = proj(cq, w_uq2, w_uq2.shape[1], tm=1024, tn=512, out_dtype=BF16, heads_batch=b)
    kv_hm = proj(ckv, w_ukv, w_ukv.shape[1], tm=1024, tn=512, out_dtype=BF16, heads_batch=b)
    s = t // b
    o = mla_attention(q_hm, kv_hm, kr.reshape(b, s, LANES), cs.reshape(b, s, LANES))
    return proj(o.reshape(t, d), w_o, d, tm=512, tn=512, out_dtype=F32, res=h)


def _sb_layer(n, h, b, w_qkv, w_o):
    t, d = n.shape
    qkv_hm = proj(n, w_qkv, w_qkv.shape[1], tm=512, tn=512, out_dtype=BF16, heads_batch=b)
    o = sb_attention(qkv_hm)
    return proj(o.reshape(t, d), w_o, d, tm=512, tn=512, out_dtype=F32, res=h)


def _dsa_layer(n, h, positions, b, w_in, idx_k_norm, w_o):
    t, d = n.shape
    s = t // b
    n_qkv = (N_HEADS + 2 * DSA_KV_HEADS) * HEAD
    n_iq = N_HEADS * HEAD
    qkv_hm = proj(n, w_in, n_qkv, tm=512, tn=512, out_dtype=BF16, heads_batch=b)
    iq_hm = proj(n, w_in, n_iq, col_off=n_qkv, tm=512, tn=512, out_dtype=BF16, heads_batch=b)
    w_tail = jnp.pad(w_in[:, n_qkv + n_iq:], ((0, 0), (0, 2 * LANES - (HEAD + N_HEADS))))
    tail = proj(n, w_tail, 2 * LANES, tm=512, tn=256, out_dtype=F32)
    bias = dsa_index(iq_hm, tail.reshape(b, s, 2 * LANES), idx_k_norm, min(DSA_TOPK_MAX, s // 4))
    slopes = jnp.exp2(-8.0 * jnp.arange(1, N_HEADS + 1, dtype=F32) / N_HEADS)
    o = dsa_attention(qkv_hm, bias, positions, slopes)
    return proj(o.reshape(t, d), w_o, d, tm=512, tn=512, out_dtype=F32, res=h)


def _rope_table(positions):
    inv = ROPE_THETA ** (-jnp.arange(0, MLA_ROPE, 2, dtype=F32) / MLA_ROPE)
    ang = positions.astype(F32)[..., None] * inv
    cos, sin = jnp.cos(ang), jnp.sin(ang)
    return jnp.concatenate([cos, cos, sin, sin], axis=-1)


@jax.jit
def kernel(x, positions, attn_norm, ffn_norm, final_norm, ffn_w_up, ffn_conv_w, ffn_conv_b, ffn_w_down,
           mla_w_a, mla_q_norm, mla_kv_norm, mla_w_uq, mla_w_ukv, mla_w_o, sb_w_qkv, sb_w_o,
           dsa_w_in, dsa_idx_k_norm, dsa_w_o):
    b, s, d = x.shape
    t = b * s
    h = x.reshape(t, d)
    cs = _rope_table(positions).reshape(t, LANES)
    for i in range(DEPTH):
        kind, j = i % N_MIXERS, i // N_MIXERS
        n = rmsnorm(h, attn_norm[i], BF16)
        if kind == 0:
            h = _mla_layer(n, h, cs, b, mla_w_a[j], mla_q_norm[j], mla_kv_norm[j], mla_w_uq[j],
                           mla_w_ukv[j], mla_w_o[j])
        elif kind == 1:
            h = _sb_layer(n, h, b, sb_w_qkv[j], sb_w_o[j])
        else:
            h = _dsa_layer(n, h, positions, b, dsa_w_in[j], dsa_idx_k_norm[j], dsa_w_o[j])
        n2 = rmsnorm(h, ffn_norm[i], BF16)
        act = ffn_up(n2, ffn_w_up[i], ffn_conv_w[i], ffn_conv_b[i], s)
        h = down_proj(act, ffn_w_down[i].astype(BF16), h)
    return rmsnorm(h, final_norm, F32).reshape(b, s, d)
```

```python
import functools

import jax
import jax.numpy as jnp
from jax import lax
from jax.experimental import pallas as pl
from jax.experimental.pallas import tpu as pltpu

F32 = jnp.float32
BF16 = jnp.bfloat16

DEPTH = 4
N_MIXERS = 3
EPS = 1e-6
HEAD = 128
N_HEADS = 32
MLA_Q_LORA = 1024
MLA_KV_LORA = 512
MLA_NOPE = 128
MLA_ROPE = 64
ROPE_THETA = 10000.0
DSA_KV_HEADS = 8
DSA_GROUP = N_HEADS // DSA_KV_HEADS
DSA_TOPK_MAX = 256
FFN_DIM = 11008
CONV_WIDTH = 3

LANES = 128
SUBLANES = 8
VMEM_CAP = 56 * 1024 * 1024
NEG_BIG = -1e30
INT_MIN = -2 ** 31


def _params(sem, vmem_bytes):
    return pltpu.CompilerParams(dimension_semantics=sem,
                                vmem_limit_bytes=int(min(VMEM_CAP, max(vmem_bytes, 16 * 1024 * 1024))))


def _dot_nt(a, b):
    return lax.dot_general(a, b, (((1,), (1,)), ((), ())), preferred_element_type=F32)


def _rmsnorm_kernel(x_ref, g_ref, o_ref):
    x = x_ref[...]
    y = x * lax.rsqrt(jnp.mean(x * x, axis=-1, keepdims=True) + EPS)
    o_ref[...] = (y * g_ref[...]).astype(o_ref.dtype)


def rmsnorm(x, g, out_dtype, tm=256):
    t, d = x.shape
    return pl.pallas_call(
        _rmsnorm_kernel,
        grid=(t // tm,),
        in_specs=[pl.BlockSpec((tm, d), lambda i: (i, 0)), pl.BlockSpec((1, d), lambda i: (0, 0))],
        out_specs=pl.BlockSpec((tm, d), lambda i: (i, 0)),
        out_shape=jax.ShapeDtypeStruct((t, d), out_dtype),
        compiler_params=_params(("parallel",), 6 * tm * d * 4),
        name="rmsnorm",
    )(x, g.reshape(1, d))


def _proj_kernel(*refs, has_res, heads_out):
    if has_res:
        x_ref, w_ref, r_ref, o_ref, wb_ref = refs
    else:
        x_ref, w_ref, o_ref, wb_ref = refs

    @pl.when(pl.program_id(1) == 0)
    def _():
        wb_ref[...] = w_ref[...].astype(BF16)

    acc = jnp.dot(x_ref[...], wb_ref[...], preferred_element_type=F32)
    if has_res:
        acc = r_ref[...] + acc
    if heads_out:
        for c in range(acc.shape[1] // HEAD):
            o_ref[0, c] = acc[:, c * HEAD:(c + 1) * HEAD].astype(o_ref.dtype)
    else:
        o_ref[...] = acc.astype(o_ref.dtype)


def proj(x, w, n_cols, *, layer=None, col_off=0, tm, tn, out_dtype, res=None, heads_batch=None):
    t, k = x.shape
    assert w.shape[-2] == k and n_cols % tn == 0 and t % tm == 0 and col_off % tn == 0
    assert (layer is None) == (w.ndim == 2)
    off_b = col_off // tn
    grid = (n_cols // tn, t // tm)
    if layer is None:
        w_spec = pl.BlockSpec((k, tn), lambda j, i: (0, j + off_b))
    else:
        w_spec = pl.BlockSpec((None, k, tn), lambda j, i: (layer, 0, j + off_b))
    in_specs = [pl.BlockSpec((tm, k), lambda j, i: (i, 0)), w_spec]
    args = [x, w]
    if res is not None:
        in_specs.append(pl.BlockSpec((tm, tn), lambda j, i: (i, j)))
        args.append(res)
    if heads_batch is None:
        out_shape = jax.ShapeDtypeStruct((t, n_cols), out_dtype)
        out_spec = pl.BlockSpec((tm, tn), lambda j, i: (i, j))
    else:
        s = t // heads_batch
        assert s % tm == 0
        nsb = s // tm
        out_shape = jax.ShapeDtypeStruct((heads_batch, n_cols // HEAD, s, HEAD), out_dtype)
        out_spec = pl.BlockSpec((1, tn // HEAD, tm, HEAD), lambda j, i: (i // nsb, j, i % nsb, 0))
    vmem = 2 * k * tn * 4 + k * tn * 2 + 2 * tm * k * 2 + 6 * tm * tn * 4
    return pl.pallas_call(
        functools.partial(_proj_kernel, has_res=res is not None, heads_out=heads_batch is not None),
        grid=grid,
        in_specs=in_specs,
        out_specs=out_spec,
        out_shape=out_shape,
        scratch_shapes=[pltpu.VMEM((k, tn), BF16)],
        compiler_params=_params(("arbitrary", "arbitrary"), vmem + (4 << 20)),
        name="proj",
    )(*args)


def _down_kernel(x_ref, w_ref, r_ref, o_ref):
    o_ref[...] = r_ref[...] + jnp.dot(x_ref[...], w_ref[...], preferred_element_type=F32)


def down_proj(x, w_bf16, layer, res, *, tm=512, tn=256):
    t, k = x.shape
    n = w_bf16.shape[2]
    vmem = 2 * tm * k * 2 + 2 * k * tn * 2 + 6 * tm * tn * 4
    return pl.pallas_call(
        _down_kernel,
        grid=(t // tm, n // tn),
        in_specs=[pl.BlockSpec((tm, k), lambda i, j: (i, 0)),
                  pl.BlockSpec((None, k, tn), lambda i, j: (layer, 0, j)),
                  pl.BlockSpec((tm, tn), lambda i, j: (i, j))],
        out_specs=pl.BlockSpec((tm, tn), lambda i, j: (i, j)),
        out_shape=jax.ShapeDtypeStruct((t, n), F32),
        compiler_params=_params(("parallel", "parallel"), vmem + (4 << 20)),
        name="down_proj",
    )(x, w_bf16, res)


def _shift_rows(u, prev, k):
    rolled = pltpu.roll(u, k, axis=0)
    top = jnp.where(lax.broadcasted_iota(jnp.int32, (SUBLANES, u.shape[1]), 0) < k,
                    pltpu.roll(prev, k, axis=0), rolled[:SUBLANES])
    return jnp.concatenate([top, rolled[SUBLANES:]], axis=0)


def _ffn_up_kernel(x_ref, wg_ref, wv_ref, cwg_ref, cwv_ref, cbg_ref, cbv_ref, o_ref,
                   wgb_ref, wvb_ref, pg_ref, pv_ref, *, tiles_per_seq, chunk):
    i = pl.program_id(1)

    @pl.when(i == 0)
    def _():
        wgb_ref[...] = wg_ref[...].astype(BF16)
        wvb_ref[...] = wv_ref[...].astype(BF16)

    @pl.when(i % tiles_per_seq == 0)
    def _():
        pg_ref[...] = jnp.zeros_like(pg_ref)
        pv_ref[...] = jnp.zeros_like(pv_ref)

    def conv(u, prev, cw_ref, cb_ref):
        c = cb_ref[...] + cw_ref[0:1, :] * _shift_rows(u, prev, 2)
        c = c + cw_ref[1:2, :] * _shift_rows(u, prev, 1)
        return c + cw_ref[2:3, :] * u

    pg, pv = pg_ref[...], pv_ref[...]
    for r in range(0, x_ref.shape[0], chunk):
        x = x_ref[r:r + chunk, :]
        ug = jnp.dot(x, wgb_ref[...], preferred_element_type=F32)
        uv = jnp.dot(x, wvb_ref[...], preferred_element_type=F32)
        gate = conv(ug, pg, cwg_ref, cbg_ref)
        val = conv(uv, pv, cwv_ref, cbv_ref)
        o_ref[r:r + chunk, :] = (gate * jax.nn.sigmoid(gate) * val).astype(o_ref.dtype)
        pg, pv = ug[chunk - SUBLANES:], uv[chunk - SUBLANES:]
    pg_ref[...] = pg
    pv_ref[...] = pv


def ffn_up(x, w_up, conv_w, conv_b, layer, seq_len, *, tm=1024, tn=256, chunk=256):
    t, k = x.shape
    f = w_up.shape[2] // 2
    nfb = f // tn
    assert f % tn == 0 and seq_len % tm == 0 and t % seq_len == 0 and tm % chunk == 0
    cb = conv_b.reshape(conv_b.shape[0], 1, 2 * f)
    vmem = 4 * k * tn * 4 + 2 * k * tn * 2 + 2 * tm * k * 2 + 12 * tm * tn * 4
    return pl.pallas_call(
        functools.partial(_ffn_up_kernel, tiles_per_seq=seq_len // tm, chunk=chunk),
        grid=(nfb, t // tm),
        in_specs=[pl.BlockSpec((tm, k), lambda j, i: (i, 0)),
                  pl.BlockSpec((None, k, tn), lambda j, i: (layer, 0, j)),
                  pl.BlockSpec((None, k, tn), lambda j, i: (layer, 0, j + nfb)),
                  pl.BlockSpec((None, CONV_WIDTH, tn), lambda j, i: (layer, 0, j)),
                  pl.BlockSpec((None, CONV_WIDTH, tn), lambda j, i: (layer, 0, j + nfb)),
                  pl.BlockSpec((None, 1, tn), lambda j, i: (layer, 0, j)),
                  pl.BlockSpec((None, 1, tn), lambda j, i: (layer, 0, j + nfb))],
        out_specs=pl.BlockSpec((tm, tn), lambda j, i: (i, j)),
        out_shape=jax.ShapeDtypeStruct((t, f), BF16),
        scratch_shapes=[pltpu.VMEM((k, tn), BF16), pltpu.VMEM((k, tn), BF16),
                        pltpu.VMEM((SUBLANES, tn), F32), pltpu.VMEM((SUBLANES, tn), F32)],
        compiler_params=_params(("arbitrary", "arbitrary"), vmem + (4 << 20)),
        name="ffn_up",
    )(x, w_up, w_up, conv_w, conv_w, cb, cb)


def _mla_prep_kernel(a_ref, cs_ref, gq_ref, gkv_ref, cq_ref, ckv_ref, kr_ref):
    def norm(v, g):
        return v * lax.rsqrt(jnp.mean(v * v, axis=-1, keepdims=True) + EPS) * g

    cq_ref[...] = norm(a_ref[:, 0:MLA_Q_LORA], gq_ref[...]).astype(cq_ref.dtype)
    ckv_ref[...] = norm(a_ref[:, MLA_Q_LORA:MLA_Q_LORA + MLA_KV_LORA], gkv_ref[...]).astype(ckv_ref.dtype)
    off = MLA_Q_LORA + MLA_KV_LORA
    y = a_ref[:, off:off + LANES] * cs_ref[...]
    kr_ref[...] = (y + pltpu.roll(y, MLA_ROPE, axis=1)).astype(kr_ref.dtype)


def mla_prep(a, cs, gq, gkv, tm=512):
    t, na = a.shape
    return pl.pallas_call(
        _mla_prep_kernel,
        grid=(t // tm,),
        in_specs=[pl.BlockSpec((tm, na), lambda i: (i, 0)),
                  pl.BlockSpec((tm, LANES), lambda i: (i, 0)),
                  pl.BlockSpec((1, MLA_Q_LORA), lambda i: (0, 0)),
                  pl.BlockSpec((1, MLA_KV_LORA), lambda i: (0, 0))],
        out_specs=[pl.BlockSpec((tm, MLA_Q_LORA), lambda i: (i, 0)),
                   pl.BlockSpec((tm, MLA_KV_LORA), lambda i: (i, 0)),
                   pl.BlockSpec((tm, LANES), lambda i: (i, 0))],
        out_shape=[jax.ShapeDtypeStruct((t, MLA_Q_LORA), BF16),
                   jax.ShapeDtypeStruct((t, MLA_KV_LORA), BF16),
                   jax.ShapeDtypeStruct((t, LANES), BF16)],
        compiler_params=_params(("parallel",), 6 * tm * na * 4),
        name="mla_prep",
    )(a, cs, gq.reshape(1, -1), gkv.reshape(1, -1))


def _softmax_step(s2, m, acc, v_ones):
    m_new = jnp.maximum(m, jnp.max(s2, axis=-1, keepdims=True))
    p = jnp.exp2(s2 - m_new).astype(BF16)
    acc = jnp.exp2(m - m_new) * acc + jnp.dot(p, v_ones, preferred_element_type=F32)
    return m_new, acc


def _softmax_init(rows):
    return jnp.full((rows, 1), -jnp.inf, F32), jnp.zeros((rows, 2 * HEAD), F32)


def _softmax_finish(acc):
    return acc[:, :HEAD] / acc[:, HEAD:]


def _mla_attn_kernel(qn_ref, qr_ref, kn_ref, v_ref, kr_ref, cs_ref, o_ref, qcat_ref, kcat_ref, vone_ref,
                     *, seq, t, scale2):
    qcat_ref[:, 0:HEAD] = qn_ref[0, 0]
    qcat_ref[:, HEAD:2 * HEAD] = (qr_ref[0, 0].astype(F32) * cs_ref[0]).astype(BF16)
    kcat_ref[:, 0:HEAD] = kn_ref[0, 0]
    kcat_ref[:, HEAD:2 * HEAD] = kr_ref[0]
    vone_ref[:, 0:HEAD] = v_ref[0, 0]
    vone_ref[:, HEAD:2 * HEAD] = jnp.ones((seq, HEAD), BF16)
    visible = (lax.broadcasted_iota(jnp.int32, (t, t), 1) <= lax.broadcasted_iota(jnp.int32, (t, t), 0))

    def q_block(i, _):
        q = qcat_ref[pl.ds(pl.multiple_of(i * t, t), t), :]

        def scores(c):
            return _dot_nt(q, kcat_ref[pl.ds(pl.multiple_of(c * t, t), t), :]) * scale2

        def values(c):
            return vone_ref[pl.ds(pl.multiple_of(c * t, t), t), :]

        def body(c, carry):
            m, acc, s2 = carry
            s2_next = scores(c + 1)
            m, acc = _softmax_step(s2, m, acc, values(c))
            return m, acc, s2_next

        m, acc, s2 = lax.fori_loop(0, i, body, _softmax_init(t) + (scores(0),))
        _, acc = _softmax_step(jnp.where(visible, s2, -jnp.inf), m, acc, values(i))
        o_ref[0, pl.ds(pl.multiple_of(i * t, t), t), :] = _softmax_finish(acc).astype(o_ref.dtype)
        return 0

    lax.fori_loop(0, seq // t, q_block, 0)


LOG2E = 1.4426950408889634


def mla_attention(q_hm, kv_hm, kr, cs, *, t=512):
    b, _, s, _ = q_hm.shape
    t = min(t, s)
    blk = (1, 1, s, HEAD)
    return pl.pallas_call(
        functools.partial(_mla_attn_kernel, seq=s, t=t, scale2=(MLA_NOPE + MLA_ROPE) ** -0.5 * LOG2E),
        grid=(b, N_HEADS),
        in_specs=[pl.BlockSpec(blk, lambda bi, h: (bi, h, 0, 0)),
                  pl.BlockSpec(blk, lambda bi, h: (bi, N_HEADS + h, 0, 0)),
                  pl.BlockSpec(blk, lambda bi, h: (bi, 2 * h, 0, 0)),
                  pl.BlockSpec(blk, lambda bi, h: (bi, 2 * h + 1, 0, 0)),
                  pl.BlockSpec((1, s, HEAD), lambda bi, h: (bi, 0, 0)),
                  pl.BlockSpec((1, s, HEAD), lambda bi, h: (bi, 0, 0))],
        out_specs=pl.BlockSpec((1, s, HEAD), lambda bi, h: (bi, 0, h)),
        out_shape=jax.ShapeDtypeStruct((b, s, N_HEADS * HEAD), BF16),
        scratch_shapes=[pltpu.VMEM((s, 2 * HEAD), BF16)] * 3,
        compiler_params=_params(("parallel", "parallel"), 40 << 20),
        name="mla_attention",
    )(q_hm, q_hm, kv_hm, kv_hm, kr, cs)


SB_EXIT_LOG = -104.0


def _sb_attn_kernel(q_ref, k_ref, v_ref, u_ref, o_ref, *, seq, t, scale, heads):
    rows = lax.broadcasted_iota(jnp.int32, (t, t), 0)
    cols = lax.broadcasted_iota(jnp.int32, (t, t), 1)
    past = cols < rows

    def tile(h, q, j, c, diag):
        k0 = pl.multiple_of(j * t, t)
        z = _dot_nt(q, k_ref[0, h, pl.ds(k0, t), :]) * scale
        log_beta = jnp.minimum(z, 0.0) - jnp.log(1.0 + jnp.exp(-jnp.abs(z)))
        log_keep = log_beta - z
        if diag:
            log_keep = jnp.where(past, log_keep, 0.0)
        hi = log_keep.astype(BF16)
        lo = (log_keep - hi.astype(F32)).astype(BF16)
        tri = u_ref[...]
        later = jnp.dot(hi, tri, preferred_element_type=F32) + jnp.dot(lo, tri, preferred_element_type=F32)
        a = jnp.exp(log_beta + later + c)
        if diag:
            a = jnp.where(past, a, 0.0)
        pv = jnp.dot(a.astype(BF16), v_ref[0, h, pl.ds(k0, t), :], preferred_element_type=F32)
        return pv, jnp.sum(log_keep, axis=-1, keepdims=True)

    def q_block(i, _):
        r0 = pl.multiple_of(i * t, t)
        qs = [q_ref[0, h, pl.ds(r0, t), :] for h in range(heads)]

        def step(j, cs, accs, diag):
            outs = [tile(h, qs[h], j, cs[h], diag) for h in range(heads)]
            return (tuple(c + o[1] for c, o in zip(cs, outs)), tuple(a + o[0] for a, o in zip(accs, outs)))

        def any_live(cs):
            top = cs[0]
            for c in cs[1:]:
                top = jnp.maximum(top, c)
            return (jnp.max(top) > SB_EXIT_LOG).astype(jnp.int32)

        zeros = lambda w: tuple(jnp.zeros((t, w), F32) for _ in range(heads))
        cs, accs = step(i, zeros(1), zeros(HEAD), True)

        def cond(carry):
            return jnp.logical_and(carry[0] >= 0, carry[3] > 0)

        def body(carry):
            j, cs, accs, _ = carry
            cs, accs = step(j, cs, accs, False)
            return j - 1, cs, accs, any_live(cs)

        _, _, accs, _ = lax.while_loop(cond, body, (i - 1, cs, accs, any_live(cs)))
        for h in range(heads):
            o_ref[0, pl.ds(r0, t), h * HEAD:(h + 1) * HEAD] = accs[h].astype(o_ref.dtype)
        return 0

    lax.fori_loop(0, seq // t, q_block, 0)


def sb_attention(qkv_hm, *, t=256, heads=2):
    b, _, s, _ = qkv_hm.shape
    blk = (1, heads, s, HEAD)
    nhb = N_HEADS // heads
    tri = (jnp.arange(t)[:, None] > jnp.arange(t)[None, :]).astype(BF16)
    return pl.pallas_call(
        functools.partial(_sb_attn_kernel, seq=s, t=t, scale=HEAD ** -0.5, heads=heads),
        grid=(b, nhb),
        in_specs=[pl.BlockSpec(blk, lambda bi, h: (bi, h, 0, 0)),
                  pl.BlockSpec(blk, lambda bi, h: (bi, nhb + h, 0, 0)),
                  pl.BlockSpec(blk, lambda bi, h: (bi, 2 * nhb + h, 0, 0)),
                  pl.BlockSpec((t, t), lambda bi, h: (0, 0))],
        out_specs=pl.BlockSpec((1, s, heads * HEAD), lambda bi, h: (bi, 0, h)),
        out_shape=jax.ShapeDtypeStruct((b, s, N_HEADS * HEAD), BF16),
        compiler_params=_params(("parallel", "parallel"), 40 << 20),
        name="sb_attention",
    )(qkv_hm, qkv_hm, qkv_hm, tri)


DSA_TQ = 128
DSA_TK = 512


def _dsa_index_kernel(iq_ref, tail_ref, iw_ref, g_ref, bias_ref, ikn_ref, key_ref,
                      *, n_chunks, top_k, w_scale, heads_per_dot):
    i = pl.program_id(1)

    @pl.when(i == 0)
    def _():
        ik = tail_ref[0]
        y = ik * lax.rsqrt(jnp.mean(ik * ik, axis=-1, keepdims=True) + EPS)
        ikn_ref[...] = (y * g_ref[...]).astype(BF16)

    n_live = (i * DSA_TQ) // DSA_TK + 1
    w = iw_ref[0][:, 0:N_HEADS] * w_scale
    rows = i * DSA_TQ + lax.broadcasted_iota(jnp.int32, (DSA_TQ, DSA_TK), 0)
    cols0 = lax.broadcasted_iota(jnp.int32, (DSA_TQ, DSA_TK), 1)

    def score_chunk(c, _):
        k0 = pl.multiple_of(c * DSA_TK, DSA_TK)
        ik = ikn_ref[pl.ds(k0, DSA_TK), :]
        sc = jnp.zeros((DSA_TQ, DSA_TK), F32)
        for hg in range(N_HEADS // heads_per_dot):
            q = iq_ref[0, hg * heads_per_dot:(hg + 1) * heads_per_dot].reshape(heads_per_dot * DSA_TQ, HEAD)
            d = _dot_nt(q, ik)
            for hh in range(heads_per_dot):
                h = hg * heads_per_dot + hh
                sc = sc + w[:, h:h + 1] * jnp.maximum(d[hh * DSA_TQ:(hh + 1) * DSA_TQ], 0.0)
        bits = pltpu.bitcast(sc, jnp.int32)
        key = jnp.where(bits < 0, bits ^ jnp.int32(0x7FFFFFFF), bits)
        key_ref[c] = jnp.where(k0 + cols0 <= rows, key, jnp.int32(INT_MIN))
        return 0

    lax.fori_loop(0, n_live, score_chunk, 0)

    def fill_dead(c, _):
        key_ref[c] = jnp.full((DSA_TQ, DSA_TK), INT_MIN, jnp.int32)
        return 0

    def bisect(n_counted):
        lax.fori_loop(n_live, n_counted, fill_dead, 0)

        def bit_step(n, thr):
            cand = thr + lax.shift_left(jnp.int32(1), jnp.int32(31) - n)
            cnt = jnp.zeros((DSA_TQ, DSA_TK), jnp.int32)
            for c in range(n_counted):
                cnt = cnt + (key_ref[c] >= cand).astype(jnp.int32)
            return jnp.where(jnp.sum(cnt, axis=-1, keepdims=True) >= top_k, cand, thr)

        return lax.fori_loop(0, 32, bit_step, jnp.full((DSA_TQ, 1), INT_MIN, jnp.int32))

    half = n_chunks // 2
    if half >= 1:
        thr = lax.cond(n_live <= half, lambda: bisect(half), lambda: bisect(n_chunks))
    else:
        thr = bisect(n_chunks)
    thr = jnp.maximum(thr, jnp.int32(INT_MIN + 1))

    def write_live(c, _):
        bias_ref[0, 0, c] = jnp.where(key_ref[c] >= thr, 0.0, NEG_BIG).astype(bias_ref.dtype)
        return 0

    def write_dead(c, _):
        bias_ref[0, 0, c] = jnp.full((DSA_TQ, DSA_TK), NEG_BIG, bias_ref.dtype)
        return 0

    lax.fori_loop(0, n_live, write_live, 0)
    lax.fori_loop(n_live, n_chunks, write_dead, 0)


def dsa_index(iq_hm, tail, g_ik, top_k):
    b, _, s, _ = iq_hm.shape
    nq, nc = s // DSA_TQ, s // DSA_TK
    w_scale = N_HEADS ** -0.5 * HEAD ** -0.5
    return pl.pallas_call(
        functools.partial(_dsa_index_kernel, n_chunks=nc, top_k=top_k, w_scale=w_scale, heads_per_dot=8),
        grid=(b, nq),
        in_specs=[pl.BlockSpec((1, N_HEADS, DSA_TQ, HEAD), lambda bi, i: (bi, 0, i, 0)),
                  pl.BlockSpec((1, s, HEAD), lambda bi, i: (bi, 0, 0)),
                  pl.BlockSpec((1, DSA_TQ, HEAD), lambda bi, i: (bi, i, 1)),
                  pl.BlockSpec((1, HEAD), lambda bi, i: (0, 0))],
        out_specs=pl.BlockSpec((1, 1, nc, DSA_TQ, DSA_TK), lambda bi, i: (bi, i, 0, 0, 0)),
        out_shape=jax.ShapeDtypeStruct((b, nq, nc, DSA_TQ, DSA_TK), BF16),
        scratch_shapes=[pltpu.VMEM((s, HEAD), BF16), pltpu.VMEM((nc, DSA_TQ, DSA_TK), jnp.int32)],
        compiler_params=_params(("arbitrary", "arbitrary"), 40 << 20),
        name="dsa_index",
    )(iq_hm, tail, tail, g_ik.reshape(1, HEAD))


def _dsa_attn_kernel(q_ref, k_ref, v_ref, bias_ref, pq_ref, pk_ref, slope_ref, o_ref, *, scale):
    i = pl.program_id(2)
    rq = DSA_GROUP * DSA_TQ
    q = q_ref[0].reshape(rq, HEAD)
    slope = slope_ref[0].reshape(DSA_GROUP, DSA_TQ, 1)
    pq = pq_ref[0]

    def step(c, carry):
        m, l, acc = carry
        k0 = pl.multiple_of(c * DSA_TK, DSA_TK)
        s = _dot_nt(q, k_ref[0, 0, pl.ds(k0, DSA_TK), :]) * scale
        dist = jnp.abs(pq - pk_ref[0, c]).astype(F32)
        pen = bias_ref[0, 0, c].astype(F32)
        s = (s.reshape(DSA_GROUP, DSA_TQ, DSA_TK) - slope * dist[None] + pen[None]).reshape(rq, DSA_TK)
        m_new = jnp.maximum(m, jnp.max(s, axis=-1, keepdims=True))
        alpha = jnp.exp(m - m_new)
        p = jnp.exp(s - m_new)
        l = alpha * l + jnp.sum(p, axis=-1, keepdims=True)
        acc = alpha * acc + jnp.dot(p.astype(BF16), v_ref[0, 0, pl.ds(k0, DSA_TK), :], preferred_element_type=F32)
        return m_new, l, acc

    init = (jnp.full((rq, 1), -jnp.inf, F32), jnp.zeros((rq, 1), F32), jnp.zeros((rq, HEAD), F32))
    _, l, acc = lax.fori_loop(0, (i * DSA_TQ) // DSA_TK + 1, step, init)
    o = acc / l
    for r in range(DSA_GROUP):
        o_ref[0, :, r * HEAD:(r + 1) * HEAD] = o[r * DSA_TQ:(r + 1) * DSA_TQ].astype(o_ref.dtype)


def dsa_attention(qkv_hm, bias, positions, slopes):
    b, _, s, _ = qkv_hm.shape
    nq, nc = s // DSA_TQ, s // DSA_TK
    kv = (1, 1, s, HEAD)
    slope_rows = jnp.repeat(slopes.reshape(DSA_KV_HEADS, DSA_GROUP), DSA_TQ, axis=1)[..., None]
    return pl.pallas_call(
        functools.partial(_dsa_attn_kernel, scale=HEAD ** -0.5),
        grid=(b, DSA_KV_HEADS, nq),
        in_specs=[pl.BlockSpec((1, DSA_GROUP, DSA_TQ, HEAD), lambda bi, g, i: (bi, g, i, 0)),
                  pl.BlockSpec(kv, lambda bi, g, i: (bi, N_HEADS + g, 0, 0)),
                  pl.BlockSpec(kv, lambda bi, g, i: (bi, N_HEADS + DSA_KV_HEADS + g, 0, 0)),
                  pl.BlockSpec((1, 1, nc, DSA_TQ, DSA_TK), lambda bi, g, i: (bi, i, 0, 0, 0)),
                  pl.BlockSpec((1, DSA_TQ, 1), lambda bi, g, i: (bi, i, 0)),
                  pl.BlockSpec((1, nc, 1, DSA_TK), lambda bi, g, i: (bi, 0, 0, 0)),
                  pl.BlockSpec((1, DSA_GROUP * DSA_TQ, 1), lambda bi, g, i: (g, 0, 0))],
        out_specs=pl.BlockSpec((1, DSA_TQ, DSA_GROUP * HEAD), lambda bi, g, i: (bi, i, g)),
        out_shape=jax.ShapeDtypeStruct((b, s, N_HEADS * HEAD), BF16),
        compiler_params=_params(("parallel", "parallel", "arbitrary"), 32 << 20),
        name="dsa_attention",
    )(qkv_hm, qkv_hm, qkv_hm, bias, positions.reshape(b, s, 1), positions.reshape(b, nc, 1, DSA_TK),
      slope_rows)


def _rotate_half_cols(r):
    half = r.shape[-1] // 2
    return jnp.concatenate([-r[..., half:], r[..., :half]], axis=-1)


def _mla_layer(n, h, cs, b, j, w_a, q_norm, kv_norm, w_uq, w_ukv, w_o):
    t, d = n.shape
    lat = MLA_Q_LORA + MLA_KV_LORA
    kr_w = w_a[j, :, lat:lat + MLA_ROPE]
    w_a2 = jnp.concatenate([w_a[j, :, :lat], kr_w, _rotate_half_cols(kr_w), jnp.zeros((d, LANES), F32)], axis=1)
    a = proj(n, w_a2, w_a2.shape[1], tm=512, tn=256, out_dtype=F32)
    cq, ckv, kr = mla_prep(a, cs, q_norm[j], kv_norm[j])
    w3 = w_uq[j].reshape(MLA_Q_LORA, N_HEADS, MLA_NOPE + MLA_ROPE)
    rope_w = w3[:, :, MLA_NOPE:]
    w_uq2 = jnp.concatenate([w3[:, :, :MLA_NOPE].reshape(MLA_Q_LORA, -1),
                             jnp.concatenate([rope_w, _rotate_half_cols(rope_w)], -1).reshape(MLA_Q_LORA, -1)], 1)
    q_hm = proj(cq, w_uq2, w_uq2.shape[1], tm=1024, tn=512, out_dtype=BF16, heads_batch=b)
    kv_hm = proj(ckv, w_ukv, w_ukv.shape[2], layer=j, tm=1024, tn=512, out_dtype=BF16, heads_batch=b)
    s = t // b
    o = mla_attention(q_hm, kv_hm, kr.reshape(b, s, LANES), cs.reshape(b, s, LANES))
    return proj(o.reshape(t, d), w_o, d, layer=j, tm=512, tn=512, out_dtype=F32, res=h)


def _sb_layer(n, h, b, j, w_qkv, w_o):
    t, d = n.shape
    qkv_hm = proj(n, w_qkv, w_qkv.shape[2], layer=j, tm=512, tn=512, out_dtype=BF16, heads_batch=b)
    o = sb_attention(qkv_hm)
    return proj(o.reshape(t, d), w_o, d, layer=j, tm=512, tn=512, out_dtype=F32, res=h)


def _dsa_layer(n, h, positions, b, j, w_in, idx_k_norm, w_o):
    t, d = n.shape
    s = t // b
    n_qkv = (N_HEADS + 2 * DSA_KV_HEADS) * HEAD
    n_iq = N_HEADS * HEAD
    qkv_hm = proj(n, w_in, n_qkv, layer=j, tm=512, tn=512, out_dtype=BF16, heads_batch=b)
    iq_hm = proj(n, w_in, n_iq, layer=j, col_off=n_qkv, tm=512, tn=512, out_dtype=BF16, heads_batch=b)
    w_tail = jnp.pad(w_in[j, :, n_qkv + n_iq:], ((0, 0), (0, 2 * LANES - (HEAD + N_HEADS))))
    tail = proj(n, w_tail, 2 * LANES, tm=512, tn=256, out_dtype=F32)
    bias = dsa_index(iq_hm, tail.reshape(b, s, 2 * LANES), idx_k_norm[j], min(DSA_TOPK_MAX, s // 4))
    slopes = jnp.exp2(-8.0 * jnp.arange(1, N_HEADS + 1, dtype=F32) / N_HEADS)
    o = dsa_attention(qkv_hm, bias, positions, slopes)
    return proj(o.reshape(t, d), w_o, d, layer=j, tm=512, tn=512, out_dtype=F32, res=h)


def _rope_table(positions):
    inv = ROPE_THETA ** (-jnp.arange(0, MLA_ROPE, 2, dtype=F32) / MLA_ROPE)
    ang = positions.astype(F32)[..., None] * inv
    cos, sin = jnp.cos(ang), jnp.sin(ang)
    return jnp.concatenate([cos, cos, sin, sin], axis=-1)


@jax.jit
def kernel(x, positions, attn_norm, ffn_norm, final_norm, ffn_w_up, ffn_conv_w, ffn_conv_b, ffn_w_down,
           mla_w_a, mla_q_norm, mla_kv_norm, mla_w_uq, mla_w_ukv, mla_w_o, sb_w_qkv, sb_w_o,
           dsa_w_in, dsa_idx_k_norm, dsa_w_o):
    b, s, d = x.shape
    t = b * s
    h = x.reshape(t, d)
    cs = _rope_table(positions).reshape(t, LANES)
    w_down = ffn_w_down.astype(BF16)
    for i in range(DEPTH):
        kind, j = i % N_MIXERS, i // N_MIXERS
        n = rmsnorm(h, attn_norm[i], BF16)
        if kind == 0:
            h = _mla_layer(n, h, cs, b, j, mla_w_a, mla_q_norm, mla_kv_norm, mla_w_uq, mla_w_ukv, mla_w_o)
        elif kind == 1:
            h = _sb_layer(n, h, b, j, sb_w_qkv, sb_w_o)
        else:
            h = _dsa_layer(n, h, positions, b, j, dsa_w_in, dsa_idx_k_norm, dsa_w_o)
        n2 = rmsnorm(h, ffn_norm[i], BF16)
        act = ffn_up(n2, ffn_w_up, ffn_conv_w, ffn_conv_b, i, s)
        h = down_proj(act, w_down, i, h)
    return rmsnorm(h, final_norm, F32).reshape(b, s, d)
```

```python
import functools

import jax
import jax.numpy as jnp
from jax import lax
from jax.experimental import pallas as pl
from jax.experimental.pallas import tpu as pltpu

F32 = jnp.float32
BF16 = jnp.bfloat16

DEPTH = 4
N_MIXERS = 3
EPS = 1e-6
HEAD = 128
N_HEADS = 32
MLA_Q_LORA = 1024
MLA_KV_LORA = 512
MLA_NOPE = 128
MLA_ROPE = 64
ROPE_THETA = 10000.0
DSA_KV_HEADS = 8
DSA_GROUP = N_HEADS // DSA_KV_HEADS
DSA_TOPK_MAX = 256
FFN_DIM = 11008
CONV_WIDTH = 3

LANES = 128
SUBLANES = 8
VMEM_CAP = 56 * 1024 * 1024
NEG_BIG = -1e30
INT_MIN = -2 ** 31


def _params(sem, vmem_bytes):
    return pltpu.CompilerParams(dimension_semantics=sem,
                                vmem_limit_bytes=int(min(VMEM_CAP, max(vmem_bytes, 16 * 1024 * 1024))))


def _dot_nt(a, b):
    return lax.dot_general(a, b, (((1,), (1,)), ((), ())), preferred_element_type=F32)


def _rmsnorm_kernel(x_ref, g_ref, o_ref):
    x = x_ref[...]
    y = x * lax.rsqrt(jnp.mean(x * x, axis=-1, keepdims=True) + EPS)
    o_ref[...] = (y * g_ref[...]).astype(o_ref.dtype)


def rmsnorm(x, g, out_dtype, tm=256):
    t, d = x.shape
    return pl.pallas_call(
        _rmsnorm_kernel,
        grid=(t // tm,),
        in_specs=[pl.BlockSpec((tm, d), lambda i: (i, 0)), pl.BlockSpec((1, d), lambda i: (0, 0))],
        out_specs=pl.BlockSpec((tm, d), lambda i: (i, 0)),
        out_shape=jax.ShapeDtypeStruct((t, d), out_dtype),
        compiler_params=_params(("parallel",), 6 * tm * d * 4),
        name="rmsnorm",
    )(x, g.reshape(1, d))


def _proj_kernel(*refs, has_res, heads_out):
    if has_res:
        x_ref, w_ref, r_ref, o_ref, wb_ref = refs
    else:
        x_ref, w_ref, o_ref, wb_ref = refs

    @pl.when(pl.program_id(1) == 0)
    def _():
        wb_ref[...] = w_ref[...].astype(BF16)

    acc = jnp.dot(x_ref[...], wb_ref[...], preferred_element_type=F32)
    if has_res:
        acc = r_ref[...] + acc
    if heads_out:
        for c in range(acc.shape[1] // HEAD):
            o_ref[0, c] = acc[:, c * HEAD:(c + 1) * HEAD].astype(o_ref.dtype)
    else:
        o_ref[...] = acc.astype(o_ref.dtype)


def proj(x, w, n_cols, *, layer=None, col_off=0, tm, tn, out_dtype, res=None, heads_batch=None):
    t, k = x.shape
    assert w.shape[-2] == k and n_cols % tn == 0 and t % tm == 0 and col_off % tn == 0
    assert (layer is None) == (w.ndim == 2)
    off_b = col_off // tn
    grid = (n_cols // tn, t // tm)
    if layer is None:
        w_spec = pl.BlockSpec((k, tn), lambda j, i: (0, j + off_b))
    else:
        w_spec = pl.BlockSpec((None, k, tn), lambda j, i: (layer, 0, j + off_b))
    in_specs = [pl.BlockSpec((tm, k), lambda j, i: (i, 0)), w_spec]
    args = [x, w]
    if res is not None:
        in_specs.append(pl.BlockSpec((tm, tn), lambda j, i: (i, j)))
        args.append(res)
    if heads_batch is None:
        out_shape = jax.ShapeDtypeStruct((t, n_cols), out_dtype)
        out_spec = pl.BlockSpec((tm, tn), lambda j, i: (i, j))
    else:
        s = t // heads_batch
        assert s % tm == 0
        nsb = s // tm
        out_shape = jax.ShapeDtypeStruct((heads_batch, n_cols // HEAD, s, HEAD), out_dtype)
        out_spec = pl.BlockSpec((1, tn // HEAD, tm, HEAD), lambda j, i: (i // nsb, j, i % nsb, 0))
    vmem = 2 * k * tn * 4 + k * tn * 2 + 2 * tm * k * 2 + 6 * tm * tn * 4
    return pl.pallas_call(
        functools.partial(_proj_kernel, has_res=res is not None, heads_out=heads_batch is not None),
        grid=grid,
        in_specs=in_specs,
        out_specs=out_spec,
        out_shape=out_shape,
        scratch_shapes=[pltpu.VMEM((k, tn), BF16)],
        compiler_params=_params(("arbitrary", "arbitrary"), vmem + (4 << 20)),
        name="proj",
    )(*args)


def _down_kernel(x_ref, w_ref, r_ref, o_ref):
    o_ref[...] = r_ref[...] + jnp.dot(x_ref[...], w_ref[...], preferred_element_type=F32)


def down_proj(x, w_bf16, layer, res, *, tm=512, tn=512):
    t, k = x.shape
    n = w_bf16.shape[2]
    vmem = 2 * tm * k * 2 + 2 * k * tn * 2 + 6 * tm * tn * 4
    return pl.pallas_call(
        _down_kernel,
        grid=(t // tm, n // tn),
        in_specs=[pl.BlockSpec((tm, k), lambda i, j: (i, 0)),
                  pl.BlockSpec((None, k, tn), lambda i, j: (layer, 0, j)),
                  pl.BlockSpec((tm, tn), lambda i, j: (i, j))],
        out_specs=pl.BlockSpec((tm, tn), lambda i, j: (i, j)),
        out_shape=jax.ShapeDtypeStruct((t, n), F32),
        compiler_params=_params(("parallel", "parallel"), vmem + (4 << 20)),
        name="down_proj",
    )(x, w_bf16, res)


def _shift_rows(u, prev, k):
    rolled = pltpu.roll(u, k, axis=0)
    top = jnp.where(lax.broadcasted_iota(jnp.int32, (SUBLANES, u.shape[1]), 0) < k,
                    pltpu.roll(prev, k, axis=0), rolled[:SUBLANES])
    return jnp.concatenate([top, rolled[SUBLANES:]], axis=0)


def _ffn_up_kernel(x_ref, wg_ref, wv_ref, cwg_ref, cwv_ref, cbg_ref, cbv_ref, o_ref,
                   wgb_ref, wvb_ref, pg_ref, pv_ref, *, tiles_per_seq, chunk):
    i = pl.program_id(1)

    @pl.when(i == 0)
    def _():
        wgb_ref[...] = wg_ref[...].astype(BF16)
        wvb_ref[...] = wv_ref[...].astype(BF16)

    @pl.when(i % tiles_per_seq == 0)
    def _():
        pg_ref[...] = jnp.zeros_like(pg_ref)
        pv_ref[...] = jnp.zeros_like(pv_ref)

    def conv(u, prev, cw_ref, cb_ref):
        c = cb_ref[...] + cw_ref[0:1, :] * _shift_rows(u, prev, 2)
        c = c + cw_ref[1:2, :] * _shift_rows(u, prev, 1)
        return c + cw_ref[2:3, :] * u

    pg, pv = pg_ref[...], pv_ref[...]
    for r in range(0, x_ref.shape[0], chunk):
        x = x_ref[r:r + chunk, :]
        ug = jnp.dot(x, wgb_ref[...], preferred_element_type=F32)
        uv = jnp.dot(x, wvb_ref[...], preferred_element_type=F32)
        gate = conv(ug, pg, cwg_ref, cbg_ref)
        val = conv(uv, pv, cwv_ref, cbv_ref)
        o_ref[r:r + chunk, :] = (gate * jax.nn.sigmoid(gate) * val).astype(o_ref.dtype)
        pg, pv = ug[chunk - SUBLANES:], uv[chunk - SUBLANES:]
    pg_ref[...] = pg
    pv_ref[...] = pv


def ffn_up(x, w_up, conv_w, conv_b, layer, seq_len, *, tm=1024, tn=256, chunk=256):
    t, k = x.shape
    f = w_up.shape[2] // 2
    nfb = f // tn
    assert f % tn == 0 and seq_len % tm == 0 and t % seq_len == 0 and tm % chunk == 0
    cb = conv_b.reshape(conv_b.shape[0], 1, 2 * f)
    vmem = 4 * k * tn * 4 + 2 * k * tn * 2 + 2 * tm * k * 2 + 12 * tm * tn * 4
    return pl.pallas_call(
        functools.partial(_ffn_up_kernel, tiles_per_seq=seq_len // tm, chunk=chunk),
        grid=(nfb, t // tm),
        in_specs=[pl.BlockSpec((tm, k), lambda j, i: (i, 0)),
                  pl.BlockSpec((None, k, tn), lambda j, i: (layer, 0, j)),
                  pl.BlockSpec((None, k, tn), lambda j, i: (layer, 0, j + nfb)),
                  pl.BlockSpec((None, CONV_WIDTH, tn), lambda j, i: (layer, 0, j)),
                  pl.BlockSpec((None, CONV_WIDTH, tn), lambda j, i: (layer, 0, j + nfb)),
                  pl.BlockSpec((None, 1, tn), lambda j, i: (layer, 0, j)),
                  pl.BlockSpec((None, 1, tn), lambda j, i: (layer, 0, j + nfb))],
        out_specs=pl.BlockSpec((tm, tn), lambda j, i: (i, j)),
        out_shape=jax.ShapeDtypeStruct((t, f), BF16),
        scratch_shapes=[pltpu.VMEM((k, tn), BF16), pltpu.VMEM((k, tn), BF16),
                        pltpu.VMEM((SUBLANES, tn), F32), pltpu.VMEM((SUBLANES, tn), F32)],
        compiler_params=_params(("arbitrary", "arbitrary"), vmem + (4 << 20)),
        name="ffn_up",
    )(x, w_up, w_up, conv_w, conv_w, cb, cb)


def _mla_prep_kernel(a_ref, cs_ref, gq_ref, gkv_ref, cq_ref, ckv_ref, kr_ref):
    def norm(v, g):
        return v * lax.rsqrt(jnp.mean(v * v, axis=-1, keepdims=True) + EPS) * g

    cq_ref[...] = norm(a_ref[:, 0:MLA_Q_LORA], gq_ref[...]).astype(cq_ref.dtype)
    ckv_ref[...] = norm(a_ref[:, MLA_Q_LORA:MLA_Q_LORA + MLA_KV_LORA], gkv_ref[...]).astype(ckv_ref.dtype)
    off = MLA_Q_LORA + MLA_KV_LORA
    y = a_ref[:, off:off + LANES] * cs_ref[...]
    kr_ref[...] = (y + pltpu.roll(y, MLA_ROPE, axis=1)).astype(kr_ref.dtype)


def mla_prep(a, cs, gq, gkv, tm=512):
    t, na = a.shape
    return pl.pallas_call(
        _mla_prep_kernel,
        grid=(t // tm,),
        in_specs=[pl.BlockSpec((tm, na), lambda i: (i, 0)),
                  pl.BlockSpec((tm, LANES), lambda i: (i, 0)),
                  pl.BlockSpec((1, MLA_Q_LORA), lambda i: (0, 0)),
                  pl.BlockSpec((1, MLA_KV_LORA), lambda i: (0, 0))],
        out_specs=[pl.BlockSpec((tm, MLA_Q_LORA), lambda i: (i, 0)),
                   pl.BlockSpec((tm, MLA_KV_LORA), lambda i: (i, 0)),
                   pl.BlockSpec((tm, LANES), lambda i: (i, 0))],
        out_shape=[jax.ShapeDtypeStruct((t, MLA_Q_LORA), BF16),
                   jax.ShapeDtypeStruct((t, MLA_KV_LORA), BF16),
                   jax.ShapeDtypeStruct((t, LANES), BF16)],
        compiler_params=_params(("parallel",), 6 * tm * na * 4),
        name="mla_prep",
    )(a, cs, gq.reshape(1, -1), gkv.reshape(1, -1))


def _softmax_step(s2, m, acc, v_ones):
    m_new = jnp.maximum(m, jnp.max(s2, axis=-1, keepdims=True))
    p = jnp.exp2(s2 - m_new).astype(BF16)
    acc = jnp.exp2(m - m_new) * acc + jnp.dot(p, v_ones, preferred_element_type=F32)
    return m_new, acc


def _softmax_init(rows):
    return jnp.full((rows, 1), -jnp.inf, F32), jnp.zeros((rows, 2 * HEAD), F32)


def _softmax_finish(acc):
    return acc[:, :HEAD] / acc[:, HEAD:]


def _mla_attn_kernel(qn_ref, qr_ref, kn_ref, v_ref, kr_ref, cs_ref, o_ref, qcat_ref, kcat_ref, vone_ref,
                     *, seq, tq, tk, scale2):
    qcat_ref[:, 0:HEAD] = qn_ref[0, 0]
    qcat_ref[:, HEAD:2 * HEAD] = (qr_ref[0, 0].astype(F32) * cs_ref[0]).astype(BF16)
    kcat_ref[:, 0:HEAD] = kn_ref[0, 0]
    kcat_ref[:, HEAD:2 * HEAD] = kr_ref[0]
    vone_ref[:, 0:HEAD] = v_ref[0, 0]
    vone_ref[:, HEAD:2 * HEAD] = jnp.ones((seq, HEAD), BF16)

    def q_block(i, _):
        r0 = pl.multiple_of(i * tq, tq)
        q = qcat_ref[pl.ds(r0, tq), :]

        def scores(c):
            return _dot_nt(q, kcat_ref[pl.ds(pl.multiple_of(c * tk, tk), tk), :]) * scale2

        def values(c):
            return vone_ref[pl.ds(pl.multiple_of(c * tk, tk), tk), :]

        def body(c, carry):
            m, acc, s2 = carry
            s2_next = scores(c + 1)
            m, acc = _softmax_step(s2, m, acc, values(c))
            return m, acc, s2_next

        n_full = (i * tq) // tk
        m, acc, s2 = lax.fori_loop(0, n_full, body, _softmax_init(tq) + (scores(0),))
        rows = r0 + lax.broadcasted_iota(jnp.int32, (tq, tk), 0)
        cols = n_full * tk + lax.broadcasted_iota(jnp.int32, (tq, tk), 1)
        _, acc = _softmax_step(jnp.where(cols <= rows, s2, -jnp.inf), m, acc, values(n_full))
        o_ref[0, pl.ds(r0, tq), :] = _softmax_finish(acc).astype(o_ref.dtype)
        return 0

    lax.fori_loop(0, seq // tq, q_block, 0)


LOG2E = 1.4426950408889634


def mla_attention(q_hm, kv_hm, kr, cs, *, tq=512, tk=1024):
    b, _, s, _ = q_hm.shape
    tk = min(tk, s)
    assert tk % tq == 0 and s % tk == 0
    blk = (1, 1, s, HEAD)
    return pl.pallas_call(
        functools.partial(_mla_attn_kernel, seq=s, tq=tq, tk=tk, scale2=(MLA_NOPE + MLA_ROPE) ** -0.5 * LOG2E),
        grid=(b, N_HEADS),
        in_specs=[pl.BlockSpec(blk, lambda bi, h: (bi, h, 0, 0)),
                  pl.BlockSpec(blk, lambda bi, h: (bi, N_HEADS + h, 0, 0)),
                  pl.BlockSpec(blk, lambda bi, h: (bi, 2 * h, 0, 0)),
                  pl.BlockSpec(blk, lambda bi, h: (bi, 2 * h + 1, 0, 0)),
                  pl.BlockSpec((1, s, HEAD), lambda bi, h: (bi, 0, 0)),
                  pl.BlockSpec((1, s, HEAD), lambda bi, h: (bi, 0, 0))],
        out_specs=pl.BlockSpec((1, s, HEAD), lambda bi, h: (bi, 0, h)),
        out_shape=jax.ShapeDtypeStruct((b, s, N_HEADS * HEAD), BF16),
        scratch_shapes=[pltpu.VMEM((s, 2 * HEAD), BF16)] * 3,
        compiler_params=_params(("parallel", "parallel"), 40 << 20),
        name="mla_attention",
    )(q_hm, q_hm, kv_hm, kv_hm, kr, cs)


SB_EXIT_LOG = -104.0


def _sb_attn_kernel(q_ref, k_ref, v_ref, u_ref, o_ref, *, seq, t, scale, heads):
    rows = lax.broadcasted_iota(jnp.int32, (t, t), 0)
    cols = lax.broadcasted_iota(jnp.int32, (t, t), 1)
    past = cols < rows

    def tile(h, q, j, c, diag):
        k0 = pl.multiple_of(j * t, t)
        z = _dot_nt(q, k_ref[0, h, pl.ds(k0, t), :]) * scale
        log_beta = jnp.minimum(z, 0.0) - jnp.log(1.0 + jnp.exp(-jnp.abs(z)))
        log_keep = log_beta - z
        if diag:
            log_keep = jnp.where(past, log_keep, 0.0)
        hi = log_keep.astype(BF16)
        lo = (log_keep - hi.astype(F32)).astype(BF16)
        tri = u_ref[...]
        later = jnp.dot(hi, tri, preferred_element_type=F32) + jnp.dot(lo, tri, preferred_element_type=F32)
        a = jnp.exp(log_beta + later + c)
        if diag:
            a = jnp.where(past, a, 0.0)
        pv = jnp.dot(a.astype(BF16), v_ref[0, h, pl.ds(k0, t), :], preferred_element_type=F32)
        return pv, jnp.sum(log_keep, axis=-1, keepdims=True)

    def q_block(i, _):
        r0 = pl.multiple_of(i * t, t)
        qs = [q_ref[0, h, pl.ds(r0, t), :] for h in range(heads)]

        def step(j, cs, accs, diag):
            outs = [tile(h, qs[h], j, cs[h], diag) for h in range(heads)]
            return (tuple(c + o[1] for c, o in zip(cs, outs)), tuple(a + o[0] for a, o in zip(accs, outs)))

        def any_live(cs):
            top = cs[0]
            for c in cs[1:]:
                top = jnp.maximum(top, c)
            return (jnp.max(top) > SB_EXIT_LOG).astype(jnp.int32)

        zeros = lambda w: tuple(jnp.zeros((t, w), F32) for _ in range(heads))
        cs, accs = step(i, zeros(1), zeros(HEAD), True)

        def cond(carry):
            return jnp.logical_and(carry[0] >= 0, carry[3] > 0)

        def body(carry):
            j, cs, accs, _ = carry
            cs, accs = step(j, cs, accs, False)
            return j - 1, cs, accs, any_live(cs)

        _, _, accs, _ = lax.while_loop(cond, body, (i - 1, cs, accs, any_live(cs)))
        for h in range(heads):
            o_ref[0, pl.ds(r0, t), h * HEAD:(h + 1) * HEAD] = accs[h].astype(o_ref.dtype)
        return 0

    lax.fori_loop(0, seq // t, q_block, 0)


def sb_attention(qkv_hm, *, t=256, heads=2):
    b, _, s, _ = qkv_hm.shape
    blk = (1, heads, s, HEAD)
    nhb = N_HEADS // heads
    tri = (jnp.arange(t)[:, None] > jnp.arange(t)[None, :]).astype(BF16)
    return pl.pallas_call(
        functools.partial(_sb_attn_kernel, seq=s, t=t, scale=HEAD ** -0.5, heads=heads),
        grid=(b, nhb),
        in_specs=[pl.BlockSpec(blk, lambda bi, h: (bi, h, 0, 0)),
                  pl.BlockSpec(blk, lambda bi, h: (bi, nhb + h, 0, 0)),
                  pl.BlockSpec(blk, lambda bi, h: (bi, 2 * nhb + h, 0, 0)),
                  pl.BlockSpec((t, t), lambda bi, h: (0, 0))],
        out_specs=pl.BlockSpec((1, s, heads * HEAD), lambda bi, h: (bi, 0, h)),
        out_shape=jax.ShapeDtypeStruct((b, s, N_HEADS * HEAD), BF16),
        compiler_params=_params(("parallel", "parallel"), 40 << 20),
        name="sb_attention",
    )(qkv_hm, qkv_hm, qkv_hm, tri)


DSA_TQ = 256
DSA_TK = 1024


def _dsa_index_kernel(iq_ref, tail_ref, iw_ref, g_ref, bias_ref, ikn_ref, key_ref,
                      *, n_chunks, top_k, w_scale, heads_per_dot):
    i = pl.program_id(1)

    @pl.when(i == 0)
    def _():
        ik = tail_ref[0]
        y = ik * lax.rsqrt(jnp.mean(ik * ik, axis=-1, keepdims=True) + EPS)
        ikn_ref[...] = (y * g_ref[...]).astype(BF16)

    n_live = (i * DSA_TQ) // DSA_TK + 1
    w = iw_ref[0][:, 0:N_HEADS] * w_scale
    rows = i * DSA_TQ + lax.broadcasted_iota(jnp.int32, (DSA_TQ, DSA_TK), 0)
    cols0 = lax.broadcasted_iota(jnp.int32, (DSA_TQ, DSA_TK), 1)

    def score_chunk(c, _):
        k0 = pl.multiple_of(c * DSA_TK, DSA_TK)
        ik = ikn_ref[pl.ds(k0, DSA_TK), :]
        sc = jnp.zeros((DSA_TQ, DSA_TK), F32)
        for hg in range(N_HEADS // heads_per_dot):
            q = iq_ref[0, hg * heads_per_dot:(hg + 1) * heads_per_dot].reshape(heads_per_dot * DSA_TQ, HEAD)
            d = _dot_nt(q, ik)
            for hh in range(heads_per_dot):
                h = hg * heads_per_dot + hh
                sc = sc + w[:, h:h + 1] * jnp.maximum(d[hh * DSA_TQ:(hh + 1) * DSA_TQ], 0.0)
        bits = pltpu.bitcast(sc, jnp.int32)
        key = jnp.where(bits < 0, bits ^ jnp.int32(0x7FFFFFFF), bits)
        key_ref[c] = jnp.where(k0 + cols0 <= rows, key, jnp.int32(INT_MIN))
        return 0

    lax.fori_loop(0, n_live, score_chunk, 0)

    def fill_dead(c, _):
        key_ref[c] = jnp.full((DSA_TQ, DSA_TK), INT_MIN, jnp.int32)
        return 0

    def bisect(n_counted):
        lax.fori_loop(n_live, n_counted, fill_dead, 0)

        def bit_step(n, thr):
            cand = thr + lax.shift_left(jnp.int32(1), jnp.int32(31) - n)
            cnt = jnp.zeros((DSA_TQ, DSA_TK), jnp.int32)
            for c in range(n_counted):
                cnt = cnt + (key_ref[c] >= cand).astype(jnp.int32)
            return jnp.where(jnp.sum(cnt, axis=-1, keepdims=True) >= top_k, cand, thr)

        return lax.fori_loop(0, 32, bit_step, jnp.full((DSA_TQ, 1), INT_MIN, jnp.int32))

    half = n_chunks // 2
    if half >= 1:
        thr = lax.cond(n_live <= half, lambda: bisect(half), lambda: bisect(n_chunks))
    else:
        thr = bisect(n_chunks)
    thr = jnp.maximum(thr, jnp.int32(INT_MIN + 1))

    def write_live(c, _):
        bias_ref[0, 0, c] = jnp.where(key_ref[c] >= thr, 0.0, NEG_BIG).astype(bias_ref.dtype)
        return 0

    def write_dead(c, _):
        bias_ref[0, 0, c] = jnp.full((DSA_TQ, DSA_TK), NEG_BIG, bias_ref.dtype)
        return 0

    lax.fori_loop(0, n_live, write_live, 0)
    lax.fori_loop(n_live, n_chunks, write_dead, 0)


def dsa_index(iq_hm, tail, g_ik, top_k):
    b, _, s, _ = iq_hm.shape
    nq, nc = s // DSA_TQ, s // DSA_TK
    w_scale = N_HEADS ** -0.5 * HEAD ** -0.5
    return pl.pallas_call(
        functools.partial(_dsa_index_kernel, n_chunks=nc, top_k=top_k, w_scale=w_scale, heads_per_dot=8),
        grid=(b, nq),
        in_specs=[pl.BlockSpec((1, N_HEADS, DSA_TQ, HEAD), lambda bi, i: (bi, 0, i, 0)),
                  pl.BlockSpec((1, s, HEAD), lambda bi, i: (bi, 0, 0)),
                  pl.BlockSpec((1, DSA_TQ, HEAD), lambda bi, i: (bi, i, 1)),
                  pl.BlockSpec((1, HEAD), lambda bi, i: (0, 0))],
        out_specs=pl.BlockSpec((1, 1, nc, DSA_TQ, DSA_TK), lambda bi, i: (bi, i, 0, 0, 0)),
        out_shape=jax.ShapeDtypeStruct((b, nq, nc, DSA_TQ, DSA_TK), BF16),
        scratch_shapes=[pltpu.VMEM((s, HEAD), BF16), pltpu.VMEM((nc, DSA_TQ, DSA_TK), jnp.int32)],
        compiler_params=_params(("arbitrary", "arbitrary"), 40 << 20),
        name="dsa_index",
    )(iq_hm, tail, tail, g_ik.reshape(1, HEAD))


def _dsa_attn_kernel(q_ref, k_ref, v_ref, bias_ref, pq_ref, pk_ref, slope_ref, o_ref, *, scale2):
    i = pl.program_id(2)
    rq = DSA_GROUP * DSA_TQ
    q = q_ref[0].reshape(rq, HEAD)
    slope = slope_ref[0].reshape(DSA_GROUP, DSA_TQ, 1)
    pq = pq_ref[0]

    def step(c, carry):
        m, l, acc = carry
        k0 = pl.multiple_of(c * DSA_TK, DSA_TK)
        raw = _dot_nt(q, k_ref[0, 0, pl.ds(k0, DSA_TK), :])
        g = bias_ref[0, 0, c].astype(F32) - jnp.abs(pq - pk_ref[0, c]).astype(F32)
        u = (raw.reshape(DSA_GROUP, DSA_TQ, DSA_TK) + slope * g[None]).reshape(rq, DSA_TK)
        m_new = jnp.maximum(m, jnp.max(u, axis=-1, keepdims=True))
        alpha = jnp.exp2((m - m_new) * scale2)
        p = jnp.exp2((u - m_new) * scale2)
        l = alpha * l + jnp.sum(p, axis=-1, keepdims=True)
        acc = alpha * acc + jnp.dot(p.astype(BF16), v_ref[0, 0, pl.ds(k0, DSA_TK), :], preferred_element_type=F32)
        return m_new, l, acc

    init = (jnp.full((rq, 1), -jnp.inf, F32), jnp.zeros((rq, 1), F32), jnp.zeros((rq, HEAD), F32))
    _, l, acc = lax.fori_loop(0, (i * DSA_TQ) // DSA_TK + 1, step, init)
    o = acc / l
    for r in range(DSA_GROUP):
        o_ref[0, :, r * HEAD:(r + 1) * HEAD] = o[r * DSA_TQ:(r + 1) * DSA_TQ].astype(o_ref.dtype)


def dsa_attention(qkv_hm, bias, positions, slopes):
    b, _, s, _ = qkv_hm.shape
    nq, nc = s // DSA_TQ, s // DSA_TK
    kv = (1, 1, s, HEAD)
    scale = HEAD ** -0.5
    slope_rows = jnp.repeat((slopes / scale).reshape(DSA_KV_HEADS, DSA_GROUP), DSA_TQ, axis=1)[..., None]
    return pl.pallas_call(
        functools.partial(_dsa_attn_kernel, scale2=scale * LOG2E),
        grid=(b, DSA_KV_HEADS, nq),
        in_specs=[pl.BlockSpec((1, DSA_GROUP, DSA_TQ, HEAD), lambda bi, g, i: (bi, g, i, 0)),
                  pl.BlockSpec(kv, lambda bi, g, i: (bi, N_HEADS + g, 0, 0)),
                  pl.BlockSpec(kv, lambda bi, g, i: (bi, N_HEADS + DSA_KV_HEADS + g, 0, 0)),
                  pl.BlockSpec((1, 1, nc, DSA_TQ, DSA_TK), lambda bi, g, i: (bi, i, 0, 0, 0)),
                  pl.BlockSpec((1, DSA_TQ, 1), lambda bi, g, i: (bi, i, 0)),
                  pl.BlockSpec((1, nc, 1, DSA_TK), lambda bi, g, i: (bi, 0, 0, 0)),
                  pl.BlockSpec((1, DSA_GROUP * DSA_TQ, 1), lambda bi, g, i: (g, 0, 0))],
        out_specs=pl.BlockSpec((1, DSA_TQ, DSA_GROUP * HEAD), lambda bi, g, i: (bi, i, g)),
        out_shape=jax.ShapeDtypeStruct((b, s, N_HEADS * HEAD), BF16),
        compiler_params=_params(("parallel", "parallel", "arbitrary"), 32 << 20),
        name="dsa_attention",
    )(qkv_hm, qkv_hm, qkv_hm, bias, positions.reshape(b, s, 1), positions.reshape(b, nc, 1, DSA_TK),
      slope_rows)


def _rotate_half_cols(r):
    half = r.shape[-1] // 2
    return jnp.concatenate([-r[..., half:], r[..., :half]], axis=-1)


def _mla_layer(n, h, cs, b, j, w_a, q_norm, kv_norm, w_uq, w_ukv, w_o):
    t, d = n.shape
    lat = MLA_Q_LORA + MLA_KV_LORA
    kr_w = w_a[j, :, lat:lat + MLA_ROPE]
    w_a2 = jnp.concatenate([w_a[j, :, :lat], kr_w, _rotate_half_cols(kr_w), jnp.zeros((d, LANES), F32)], axis=1)
    a = proj(n, w_a2, w_a2.shape[1], tm=1024, tn=256, out_dtype=F32)
    cq, ckv, kr = mla_prep(a, cs, q_norm[j], kv_norm[j])
    w3 = w_uq[j].reshape(MLA_Q_LORA, N_HEADS, MLA_NOPE + MLA_ROPE)
    rope_w = w3[:, :, MLA_NOPE:]
    w_uq2 = jnp.concatenate([w3[:, :, :MLA_NOPE].reshape(MLA_Q_LORA, -1),
                             jnp.concatenate([rope_w, _rotate_half_cols(rope_w)], -1).reshape(MLA_Q_LORA, -1)], 1)
    q_hm = proj(cq, w_uq2, w_uq2.shape[1], tm=1024, tn=512, out_dtype=BF16, heads_batch=b)
    kv_hm = proj(ckv, w_ukv, w_ukv.shape[2], layer=j, tm=1024, tn=512, out_dtype=BF16, heads_batch=b)
    s = t // b
    o = mla_attention(q_hm, kv_hm, kr.reshape(b, s, LANES), cs.reshape(b, s, LANES))
    return proj(o.reshape(t, d), w_o, d, layer=j, tm=1024, tn=512, out_dtype=F32, res=h)


def _sb_layer(n, h, b, j, w_qkv, w_o):
    t, d = n.shape
    qkv_hm = proj(n, w_qkv, w_qkv.shape[2], layer=j, tm=1024, tn=512, out_dtype=BF16, heads_batch=b)
    o = sb_attention(qkv_hm)
    return proj(o.reshape(t, d), w_o, d, layer=j, tm=1024, tn=512, out_dtype=F32, res=h)


def _dsa_layer(n, h, positions, b, j, w_in, idx_k_norm, w_o):
    t, d = n.shape
    s = t // b
    n_qkv = (N_HEADS + 2 * DSA_KV_HEADS) * HEAD
    n_iq = N_HEADS * HEAD
    qkv_hm = proj(n, w_in, n_qkv, layer=j, tm=1024, tn=512, out_dtype=BF16, heads_batch=b)
    iq_hm = proj(n, w_in, n_iq, layer=j, col_off=n_qkv, tm=1024, tn=512, out_dtype=BF16, heads_batch=b)
    w_tail = jnp.pad(w_in[j, :, n_qkv + n_iq:], ((0, 0), (0, 2 * LANES - (HEAD + N_HEADS))))
    tail = proj(n, w_tail, 2 * LANES, tm=1024, tn=256, out_dtype=F32)
    bias = dsa_index(iq_hm, tail.reshape(b, s, 2 * LANES), idx_k_norm[j], min(DSA_TOPK_MAX, s // 4))
    slopes = jnp.exp2(-8.0 * jnp.arange(1, N_HEADS + 1, dtype=F32) / N_HEADS)
    o = dsa_attention(qkv_hm, bias, positions, slopes)
    return proj(o.reshape(t, d), w_o, d, layer=j, tm=1024, tn=512, out_dtype=F32, res=h)


def _rope_table(positions):
    inv = ROPE_THETA ** (-jnp.arange(0, MLA_ROPE, 2, dtype=F32) / MLA_ROPE)
    ang = positions.astype(F32)[..., None] * inv
    cos, sin = jnp.cos(ang), jnp.sin(ang)
    return jnp.concatenate([cos, cos, sin, sin], axis=-1)


@jax.jit
def kernel(x, positions, attn_norm, ffn_norm, final_norm, ffn_w_up, ffn_conv_w, ffn_conv_b, ffn_w_down,
           mla_w_a, mla_q_norm, mla_kv_norm, mla_w_uq, mla_w_ukv, mla_w_o, sb_w_qkv, sb_w_o,
           dsa_w_in, dsa_idx_k_norm, dsa_w_o):
    b, s, d = x.shape
    t = b * s
    h = x.reshape(t, d)
    cs = _rope_table(positions).reshape(t, LANES)
    w_down = ffn_w_down.astype(BF16)
    for i in range(DEPTH):
        kind, j = i % N_MIXERS, i // N_MIXERS
        n = rmsnorm(h, attn_norm[i], BF16)
        if kind == 0:
            h = _mla_layer(n, h, cs, b, j, mla_w_a, mla_q_norm, mla_kv_norm, mla_w_uq, mla_w_ukv, mla_w_o)
        elif kind == 1:
            h = _sb_layer(n, h, b, j, sb_w_qkv, sb_w_o)
        else:
            h = _dsa_layer(n, h, positions, b, j, dsa_w_in, dsa_idx_k_norm, dsa_w_o)
        n2 = rmsnorm(h, ffn_norm[i], BF16)
        act = ffn_up(n2, ffn_w_up, ffn_conv_w, ffn_conv_b, i, s)
        h = down_proj(act, w_down, i, h)
    return rmsnorm(h, final_norm, F32).reshape(b, s, d)
```

```python
import functools

import jax
import jax.numpy as jnp
from jax import lax
from jax.experimental import pallas as pl
from jax.experimental.pallas import tpu as pltpu

F32 = jnp.float32
BF16 = jnp.bfloat16

DEPTH = 4
N_MIXERS = 3
EPS = 1e-6
HEAD = 128
N_HEADS = 32
MLA_Q_LORA = 1024
MLA_KV_LORA = 512
MLA_NOPE = 128
MLA_ROPE = 64
ROPE_THETA = 10000.0
DSA_KV_HEADS = 8
DSA_GROUP = N_HEADS // DSA_KV_HEADS
DSA_TOPK_MAX = 256
FFN_DIM = 11008
CONV_WIDTH = 3

LANES = 128
SUBLANES = 8
VMEM_CAP = 56 * 1024 * 1024
NEG_BIG = -1e30
INT_MIN = -2 ** 31


def _params(sem, vmem_bytes):
    return pltpu.CompilerParams(dimension_semantics=sem,
                                vmem_limit_bytes=int(min(VMEM_CAP, max(vmem_bytes, 16 * 1024 * 1024))))


def _dot_nt(a, b):
    return lax.dot_general(a, b, (((1,), (1,)), ((), ())), preferred_element_type=F32)


def _rmsnorm_kernel(x_ref, g_ref, o_ref):
    x = x_ref[...]
    y = x * lax.rsqrt(jnp.mean(x * x, axis=-1, keepdims=True) + EPS)
    o_ref[...] = (y * g_ref[...]).astype(o_ref.dtype)


def rmsnorm(x, g, out_dtype, tm=256):
    t, d = x.shape
    return pl.pallas_call(
        _rmsnorm_kernel,
        grid=(t // tm,),
        in_specs=[pl.BlockSpec((tm, d), lambda i: (i, 0)), pl.BlockSpec((1, d), lambda i: (0, 0))],
        out_specs=pl.BlockSpec((tm, d), lambda i: (i, 0)),
        out_shape=jax.ShapeDtypeStruct((t, d), out_dtype),
        compiler_params=_params(("parallel",), 6 * tm * d * 4),
        name="rmsnorm",
    )(x, g.reshape(1, d))


def _proj_kernel(*refs, has_res, heads_out):
    if has_res:
        x_ref, w_ref, r_ref, o_ref, wb_ref = refs
    else:
        x_ref, w_ref, o_ref, wb_ref = refs

    @pl.when(pl.program_id(1) == 0)
    def _():
        wb_ref[...] = w_ref[...].astype(BF16)

    acc = jnp.dot(x_ref[...], wb_ref[...], preferred_element_type=F32)
    if has_res:
        acc = r_ref[...] + acc
    if heads_out:
        for c in range(acc.shape[1] // HEAD):
            o_ref[0, c] = acc[:, c * HEAD:(c + 1) * HEAD].astype(o_ref.dtype)
    else:
        o_ref[...] = acc.astype(o_ref.dtype)


def proj(x, w, n_cols, *, layer=None, col_off=0, tm, tn, out_dtype, res=None, heads_batch=None):
    t, k = x.shape
    assert w.shape[-2] == k and n_cols % tn == 0 and t % tm == 0 and col_off % tn == 0
    assert (layer is None) == (w.ndim == 2)
    off_b = col_off // tn
    grid = (n_cols // tn, t // tm)
    if layer is None:
        w_spec = pl.BlockSpec((k, tn), lambda j, i: (0, j + off_b))
    else:
        w_spec = pl.BlockSpec((None, k, tn), lambda j, i: (layer, 0, j + off_b))
    in_specs = [pl.BlockSpec((tm, k), lambda j, i: (i, 0)), w_spec]
    args = [x, w]
    if res is not None:
        in_specs.append(pl.BlockSpec((tm, tn), lambda j, i: (i, j)))
        args.append(res)
    if heads_batch is None:
        out_shape = jax.ShapeDtypeStruct((t, n_cols), out_dtype)
        out_spec = pl.BlockSpec((tm, tn), lambda j, i: (i, j))
    else:
        s = t // heads_batch
        assert s % tm == 0
        nsb = s // tm
        out_shape = jax.ShapeDtypeStruct((heads_batch, n_cols // HEAD, s, HEAD), out_dtype)
        out_spec = pl.BlockSpec((1, tn // HEAD, tm, HEAD), lambda j, i: (i // nsb, j, i % nsb, 0))
    vmem = 2 * k * tn * 4 + k * tn * 2 + 2 * tm * k * 2 + 6 * tm * tn * 4
    return pl.pallas_call(
        functools.partial(_proj_kernel, has_res=res is not None, heads_out=heads_batch is not None),
        grid=grid,
        in_specs=in_specs,
        out_specs=out_spec,
        out_shape=out_shape,
        scratch_shapes=[pltpu.VMEM((k, tn), BF16)],
        compiler_params=_params(("arbitrary", "arbitrary"), vmem + (4 << 20)),
        name="proj",
    )(*args)


def _down_kernel(x_ref, w_ref, r_ref, o_ref):
    o_ref[...] = r_ref[...] + jnp.dot(x_ref[...], w_ref[...], preferred_element_type=F32)


def down_proj(x, w_bf16, res, *, tm=512, tn=512):
    t, k = x.shape
    n = w_bf16.shape[1]
    vmem = 2 * tm * k * 2 + 2 * k * tn * 2 + 6 * tm * tn * 4
    return pl.pallas_call(
        _down_kernel,
        grid=(t // tm, n // tn),
        in_specs=[pl.BlockSpec((tm, k), lambda i, j: (i, 0)),
                  pl.BlockSpec((k, tn), lambda i, j: (0, j)),
                  pl.BlockSpec((tm, tn), lambda i, j: (i, j))],
        out_specs=pl.BlockSpec((tm, tn), lambda i, j: (i, j)),
        out_shape=jax.ShapeDtypeStruct((t, n), F32),
        compiler_params=_params(("parallel", "parallel"), vmem + (4 << 20)),
        name="down_proj",
    )(x, w_bf16, res)


def _shift_rows(u, prev, k):
    rolled = pltpu.roll(u, k, axis=0)
    top = jnp.where(lax.broadcasted_iota(jnp.int32, (SUBLANES, u.shape[1]), 0) < k,
                    pltpu.roll(prev, k, axis=0), rolled[:SUBLANES])
    return jnp.concatenate([top, rolled[SUBLANES:]], axis=0)


def _ffn_up_kernel(x_ref, wg_ref, wv_ref, cwg_ref, cwv_ref, cbg_ref, cbv_ref, wd_ref, o_ref, wdb_ref,
                   wgb_ref, wvb_ref, pg_ref, pv_ref, *, tiles_per_seq, chunk):
    i = pl.program_id(1)

    wdb_ref[...] = wd_ref[...].astype(BF16)

    @pl.when(i == 0)
    def _():
        wgb_ref[...] = wg_ref[...].astype(BF16)
        wvb_ref[...] = wv_ref[...].astype(BF16)

    @pl.when(i % tiles_per_seq == 0)
    def _():
        pg_ref[...] = jnp.zeros_like(pg_ref)
        pv_ref[...] = jnp.zeros_like(pv_ref)

    def conv(u, prev, cw_ref, cb_ref):
        c = cb_ref[...] + cw_ref[0:1, :] * _shift_rows(u, prev, 2)
        c = c + cw_ref[1:2, :] * _shift_rows(u, prev, 1)
        return c + cw_ref[2:3, :] * u

    pg, pv = pg_ref[...], pv_ref[...]
    for r in range(0, x_ref.shape[0], chunk):
        x = x_ref[r:r + chunk, :]
        ug = jnp.dot(x, wgb_ref[...], preferred_element_type=F32)
        uv = jnp.dot(x, wvb_ref[...], preferred_element_type=F32)
        gate = conv(ug, pg, cwg_ref, cbg_ref)
        val = conv(uv, pv, cwv_ref, cbv_ref)
        o_ref[r:r + chunk, :] = (gate * jax.nn.sigmoid(gate) * val).astype(o_ref.dtype)
        pg, pv = ug[chunk - SUBLANES:], uv[chunk - SUBLANES:]
    pg_ref[...] = pg
    pv_ref[...] = pv


def ffn_up(x, w_up, conv_w, conv_b, w_down, layer, seq_len, *, tm=1024, tn=256, chunk=256):
    t, k = x.shape
    f = w_up.shape[2] // 2
    nfb, nmt = f // tn, t // tm
    assert f % tn == 0 and seq_len % tm == 0 and t % seq_len == 0 and tm % chunk == 0
    kd, nd = w_down.shape[1:]
    slab = kd // (nfb * nmt)
    assert slab * nfb * nmt == kd and slab % (2 * SUBLANES) == 0
    cb = conv_b.reshape(conv_b.shape[0], 1, 2 * f)
    vmem = 4 * k * tn * 4 + 2 * k * tn * 2 + 2 * tm * k * 2 + 12 * tm * tn * 4 + 12 * slab * nd
    return pl.pallas_call(
        functools.partial(_ffn_up_kernel, tiles_per_seq=seq_len // tm, chunk=chunk),
        grid=(nfb, nmt),
        in_specs=[pl.BlockSpec((tm, k), lambda j, i: (i, 0)),
                  pl.BlockSpec((None, k, tn), lambda j, i: (layer, 0, j)),
                  pl.BlockSpec((None, k, tn), lambda j, i: (layer, 0, j + nfb)),
                  pl.BlockSpec((None, CONV_WIDTH, tn), lambda j, i: (layer, 0, j)),
                  pl.BlockSpec((None, CONV_WIDTH, tn), lambda j, i: (layer, 0, j + nfb)),
                  pl.BlockSpec((None, 1, tn), lambda j, i: (layer, 0, j)),
                  pl.BlockSpec((None, 1, tn), lambda j, i: (layer, 0, j + nfb)),
                  pl.BlockSpec((None, slab, nd), lambda j, i: (layer, j * nmt + i, 0))],
        out_specs=[pl.BlockSpec((tm, tn), lambda j, i: (i, j)),
                   pl.BlockSpec((slab, nd), lambda j, i: (j * nmt + i, 0))],
        out_shape=[jax.ShapeDtypeStruct((t, f), BF16), jax.ShapeDtypeStruct((kd, nd), BF16)],
        scratch_shapes=[pltpu.VMEM((k, tn), BF16), pltpu.VMEM((k, tn), BF16),
                        pltpu.VMEM((SUBLANES, tn), F32), pltpu.VMEM((SUBLANES, tn), F32)],
        compiler_params=_params(("arbitrary", "arbitrary"), vmem + (4 << 20)),
        name="ffn_up",
    )(x, w_up, w_up, conv_w, conv_w, cb, cb, w_down)


def _mla_prep_kernel(a_ref, cs_ref, gq_ref, gkv_ref, cq_ref, ckv_ref, kr_ref):
    def norm(v, g):
        return v * lax.rsqrt(jnp.mean(v * v, axis=-1, keepdims=True) + EPS) * g

    cq_ref[...] = norm(a_ref[:, 0:MLA_Q_LORA], gq_ref[...]).astype(cq_ref.dtype)
    ckv_ref[...] = norm(a_ref[:, MLA_Q_LORA:MLA_Q_LORA + MLA_KV_LORA], gkv_ref[...]).astype(ckv_ref.dtype)
    off = MLA_Q_LORA + MLA_KV_LORA
    y = a_ref[:, off:off + LANES] * cs_ref[...]
    kr_ref[...] = (y + pltpu.roll(y, MLA_ROPE, axis=1)).astype(kr_ref.dtype)


def mla_prep(a, cs, gq, gkv, tm=512):
    t, na = a.shape
    return pl.pallas_call(
        _mla_prep_kernel,
        grid=(t // tm,),
        in_specs=[pl.BlockSpec((tm, na), lambda i: (i, 0)),
                  pl.BlockSpec((tm, LANES), lambda i: (i, 0)),
                  pl.BlockSpec((1, MLA_Q_LORA), lambda i: (0, 0)),
                  pl.BlockSpec((1, MLA_KV_LORA), lambda i: (0, 0))],
        out_specs=[pl.BlockSpec((tm, MLA_Q_LORA), lambda i: (i, 0)),
                   pl.BlockSpec((tm, MLA_KV_LORA), lambda i: (i, 0)),
                   pl.BlockSpec((tm, LANES), lambda i: (i, 0))],
        out_shape=[jax.ShapeDtypeStruct((t, MLA_Q_LORA), BF16),
                   jax.ShapeDtypeStruct((t, MLA_KV_LORA), BF16),
                   jax.ShapeDtypeStruct((t, LANES), BF16)],
        compiler_params=_params(("parallel",), 6 * tm * na * 4),
        name="mla_prep",
    )(a, cs, gq.reshape(1, -1), gkv.reshape(1, -1))


def _softmax_step(s2, m, acc, v_ones):
    m_new = jnp.maximum(m, jnp.max(s2, axis=-1, keepdims=True))
    p = jnp.exp2(s2 - m_new).astype(BF16)
    acc = jnp.exp2(m - m_new) * acc + jnp.dot(p, v_ones, preferred_element_type=F32)
    return m_new, acc


def _softmax_init(rows):
    return jnp.full((rows, 1), -jnp.inf, F32), jnp.zeros((rows, 2 * HEAD), F32)


def _softmax_finish(acc):
    return acc[:, :HEAD] / acc[:, HEAD:]


def _mla_attn_kernel(qn_ref, qr_ref, kn_ref, v_ref, kr_ref, cs_ref, o_ref, qcat_ref, kcat_ref, vone_ref,
                     *, seq, tq, tk, scale2):
    qcat_ref[:, 0:HEAD] = qn_ref[0, 0]
    qcat_ref[:, HEAD:2 * HEAD] = (qr_ref[0, 0].astype(F32) * cs_ref[0]).astype(BF16)
    kcat_ref[:, 0:HEAD] = kn_ref[0, 0]
    kcat_ref[:, HEAD:2 * HEAD] = kr_ref[0]
    vone_ref[:, 0:HEAD] = v_ref[0, 0]
    vone_ref[:, HEAD:2 * HEAD] = jnp.ones((seq, HEAD), BF16)

    def q_block(i, _):
        r0 = pl.multiple_of(i * tq, tq)
        q = qcat_ref[pl.ds(r0, tq), :]

        def scores(c):
            return _dot_nt(q, kcat_ref[pl.ds(pl.multiple_of(c * tk, tk), tk), :]) * scale2

        def values(c):
            return vone_ref[pl.ds(pl.multiple_of(c * tk, tk), tk), :]

        def body(c, carry):
            m, acc, s2 = carry
            s2_next = scores(c + 1)
            m, acc = _softmax_step(s2, m, acc, values(c))
            return m, acc, s2_next

        n_full = (i * tq) // tk
        m, acc, s2 = lax.fori_loop(0, n_full, body, _softmax_init(tq) + (scores(0),))
        rows = r0 + lax.broadcasted_iota(jnp.int32, (tq, tk), 0)
        cols = n_full * tk + lax.broadcasted_iota(jnp.int32, (tq, tk), 1)
        _, acc = _softmax_step(jnp.where(cols <= rows, s2, -jnp.inf), m, acc, values(n_full))
        o_ref[0, pl.ds(r0, tq), :] = _softmax_finish(acc).astype(o_ref.dtype)
        return 0

    lax.fori_loop(0, seq // tq, q_block, 0)


LOG2E = 1.4426950408889634


def mla_attention(q_hm, kv_hm, kr, cs, *, tq=512, tk=1024):
    b, _, s, _ = q_hm.shape
    tk = min(tk, s)
    assert tk % tq == 0 and s % tk == 0
    blk = (1, 1, s, HEAD)
    return pl.pallas_call(
        functools.partial(_mla_attn_kernel, seq=s, tq=tq, tk=tk, scale2=(MLA_NOPE + MLA_ROPE) ** -0.5 * LOG2E),
        grid=(b, N_HEADS),
        in_specs=[pl.BlockSpec(blk, lambda bi, h: (bi, h, 0, 0)),
                  pl.BlockSpec(blk, lambda bi, h: (bi, N_HEADS + h, 0, 0)),
                  pl.BlockSpec(blk, lambda bi, h: (bi, 2 * h, 0, 0)),
                  pl.BlockSpec(blk, lambda bi, h: (bi, 2 * h + 1, 0, 0)),
                  pl.BlockSpec((1, s, HEAD), lambda bi, h: (bi, 0, 0)),
                  pl.BlockSpec((1, s, HEAD), lambda bi, h: (bi, 0, 0))],
        out_specs=pl.BlockSpec((1, s, HEAD), lambda bi, h: (bi, 0, h)),
        out_shape=jax.ShapeDtypeStruct((b, s, N_HEADS * HEAD), BF16),
        scratch_shapes=[pltpu.VMEM((s, 2 * HEAD), BF16)] * 3,
        compiler_params=_params(("parallel", "parallel"), 40 << 20),
        name="mla_attention",
    )(q_hm, q_hm, kv_hm, kv_hm, kr, cs)


SB_EXIT_LOG = -104.0


def _sb_attn_kernel(q_ref, k_ref, v_ref, u_ref, o_ref, *, seq, t, scale, heads):
    rows = lax.broadcasted_iota(jnp.int32, (t, t), 0)
    cols = lax.broadcasted_iota(jnp.int32, (t, t), 1)
    past = cols < rows

    def tile(h, q, j, c, diag):
        k0 = pl.multiple_of(j * t, t)
        z = _dot_nt(q, k_ref[0, h, pl.ds(k0, t), :]) * scale
        log_beta = jnp.minimum(z, 0.0) - jnp.log(1.0 + jnp.exp(-jnp.abs(z)))
        log_keep = log_beta - z
        if diag:
            log_keep = jnp.where(past, log_keep, 0.0)
        hi = log_keep.astype(BF16)
        lo = (log_keep - hi.astype(F32)).astype(BF16)
        tri = u_ref[...]
        later = jnp.dot(hi, tri, preferred_element_type=F32) + jnp.dot(lo, tri, preferred_element_type=F32)
        a = jnp.exp(log_beta + later + c)
        if diag:
            a = jnp.where(past, a, 0.0)
        pv = jnp.dot(a.astype(BF16), v_ref[0, h, pl.ds(k0, t), :], preferred_element_type=F32)
        return pv, jnp.sum(log_keep, axis=-1, keepdims=True)

    def q_block(i, _):
        r0 = pl.multiple_of(i * t, t)
        qs = [q_ref[0, h, pl.ds(r0, t), :] for h in range(heads)]

        def step(j, cs, accs, diag):
            outs = [tile(h, qs[h], j, cs[h], diag) for h in range(heads)]
            return (tuple(c + o[1] for c, o in zip(cs, outs)), tuple(a + o[0] for a, o in zip(accs, outs)))

        def any_live(cs):
            top = cs[0]
            for c in cs[1:]:
                top = jnp.maximum(top, c)
            return (jnp.max(top) > SB_EXIT_LOG).astype(jnp.int32)

        zeros = lambda w: tuple(jnp.zeros((t, w), F32) for _ in range(heads))
        cs, accs = step(i, zeros(1), zeros(HEAD), True)

        def cond(carry):
            return jnp.logical_and(carry[0] >= 0, carry[3] > 0)

        def body(carry):
            j, cs, accs, _ = carry
            cs, accs = step(j, cs, accs, False)
            return j - 1, cs, accs, any_live(cs)

        _, _, accs, _ = lax.while_loop(cond, body, (i - 1, cs, accs, any_live(cs)))
        for h in range(heads):
            o_ref[0, pl.ds(r0, t), h * HEAD:(h + 1) * HEAD] = accs[h].astype(o_ref.dtype)
        return 0

    lax.fori_loop(0, seq // t, q_block, 0)


def sb_attention(qkv_hm, *, t=256, heads=4):
    b, _, s, _ = qkv_hm.shape
    blk = (1, heads, s, HEAD)
    nhb = N_HEADS // heads
    tri = (jnp.arange(t)[:, None] > jnp.arange(t)[None, :]).astype(BF16)
    return pl.pallas_call(
        functools.partial(_sb_attn_kernel, seq=s, t=t, scale=HEAD ** -0.5, heads=heads),
        grid=(b, nhb),
        in_specs=[pl.BlockSpec(blk, lambda bi, h: (bi, h, 0, 0)),
                  pl.BlockSpec(blk, lambda bi, h: (bi, nhb + h, 0, 0)),
                  pl.BlockSpec(blk, lambda bi, h: (bi, 2 * nhb + h, 0, 0)),
                  pl.BlockSpec((t, t), lambda bi, h: (0, 0))],
        out_specs=pl.BlockSpec((1, s, heads * HEAD), lambda bi, h: (bi, 0, h)),
        out_shape=jax.ShapeDtypeStruct((b, s, N_HEADS * HEAD), BF16),
        compiler_params=_params(("parallel", "parallel"), 40 << 20),
        name="sb_attention",
    )(qkv_hm, qkv_hm, qkv_hm, tri)


DSA_TQ = 256
DSA_TK = 1024


def _dsa_index_kernel(iq_ref, tail_ref, iw_ref, g_ref, bias_ref, ikn_ref, key_ref,
                      *, n_chunks, top_k, w_scale, heads_per_dot):
    i = pl.program_id(1)

    @pl.when(i == 0)
    def _():
        ik = tail_ref[0]
        y = ik * lax.rsqrt(jnp.mean(ik * ik, axis=-1, keepdims=True) + EPS)
        ikn_ref[...] = (y * g_ref[...]).astype(BF16)

    n_live = (i * DSA_TQ) // DSA_TK + 1
    w = iw_ref[0][:, 0:N_HEADS] * w_scale
    rows = i * DSA_TQ + lax.broadcasted_iota(jnp.int32, (DSA_TQ, DSA_TK), 0)
    cols0 = lax.broadcasted_iota(jnp.int32, (DSA_TQ, DSA_TK), 1)

    def score_chunk(c, _):
        k0 = pl.multiple_of(c * DSA_TK, DSA_TK)
        ik = ikn_ref[pl.ds(k0, DSA_TK), :]
        sc = jnp.zeros((DSA_TQ, DSA_TK), F32)
        for hg in range(N_HEADS // heads_per_dot):
            q = iq_ref[0, hg * heads_per_dot:(hg + 1) * heads_per_dot].reshape(heads_per_dot * DSA_TQ, HEAD)
            d = _dot_nt(q, ik)
            for hh in range(heads_per_dot):
                h = hg * heads_per_dot + hh
                sc = sc + w[:, h:h + 1] * jnp.maximum(d[hh * DSA_TQ:(hh + 1) * DSA_TQ], 0.0)
        bits = pltpu.bitcast(sc, jnp.int32)
        key = jnp.where(bits < 0, bits ^ jnp.int32(0x7FFFFFFF), bits)
        key_ref[c] = jnp.where(k0 + cols0 <= rows, key, jnp.int32(INT_MIN))
        return 0

    lax.fori_loop(0, n_live, score_chunk, 0)

    def fill_dead(c, _):
        key_ref[c] = jnp.full((DSA_TQ, DSA_TK), INT_MIN, jnp.int32)
        return 0

    def bisect(n_counted):
        lax.fori_loop(n_live, n_counted, fill_dead, 0)

        def bit_step(n, thr):
            cand = thr + lax.shift_left(jnp.int32(1), jnp.int32(31) - n)
            cnt = jnp.zeros((DSA_TQ, DSA_TK), jnp.int32)
            for c in range(n_counted):
                cnt = cnt + (key_ref[c] >= cand).astype(jnp.int32)
            return jnp.where(jnp.sum(cnt, axis=-1, keepdims=True) >= top_k, cand, thr)

        return lax.fori_loop(0, 32, bit_step, jnp.full((DSA_TQ, 1), INT_MIN, jnp.int32))

    half = n_chunks // 2
    if half >= 1:
        thr = lax.cond(n_live <= half, lambda: bisect(half), lambda: bisect(n_chunks))
    else:
        thr = bisect(n_chunks)
    thr = jnp.maximum(thr, jnp.int32(INT_MIN + 1))

    def write_live(c, _):
        bias_ref[0, 0, c] = jnp.where(key_ref[c] >= thr, 0.0, NEG_BIG).astype(bias_ref.dtype)
        return 0

    def write_dead(c, _):
        bias_ref[0, 0, c] = jnp.full((DSA_TQ, DSA_TK), NEG_BIG, bias_ref.dtype)
        return 0

    lax.fori_loop(0, n_live, write_live, 0)
    lax.fori_loop(n_live, n_chunks, write_dead, 0)


def dsa_index(iq_hm, tail, g_ik, top_k):
    b, _, s, _ = iq_hm.shape
    nq, nc = s // DSA_TQ, s // DSA_TK
    w_scale = N_HEADS ** -0.5 * HEAD ** -0.5
    return pl.pallas_call(
        functools.partial(_dsa_index_kernel, n_chunks=nc, top_k=top_k, w_scale=w_scale, heads_per_dot=8),
        grid=(b, nq),
        in_specs=[pl.BlockSpec((1, N_HEADS, DSA_TQ, HEAD), lambda bi, i: (bi, 0, i, 0)),
                  pl.BlockSpec((1, s, HEAD), lambda bi, i: (bi, 0, 0)),
                  pl.BlockSpec((1, DSA_TQ, HEAD), lambda bi, i: (bi, i, 1)),
                  pl.BlockSpec((1, HEAD), lambda bi, i: (0, 0))],
        out_specs=pl.BlockSpec((1, 1, nc, DSA_TQ, DSA_TK), lambda bi, i: (bi, i, 0, 0, 0)),
        out_shape=jax.ShapeDtypeStruct((b, nq, nc, DSA_TQ, DSA_TK), BF16),
        scratch_shapes=[pltpu.VMEM((s, HEAD), BF16), pltpu.VMEM((nc, DSA_TQ, DSA_TK), jnp.int32)],
        compiler_params=_params(("arbitrary", "arbitrary"), 40 << 20),
        name="dsa_index",
    )(iq_hm, tail, tail, g_ik.reshape(1, HEAD))


def _dsa_attn_kernel(q_ref, k_ref, v_ref, bias_ref, pq_ref, pk_ref, slope_ref, o_ref, *, scale2):
    i = pl.program_id(2)
    rq = DSA_GROUP * DSA_TQ
    q = q_ref[0].reshape(rq, HEAD)
    slope = slope_ref[0].reshape(DSA_GROUP, DSA_TQ, 1)
    pq = pq_ref[0]

    def step(c, carry):
        m, l, acc = carry
        k0 = pl.multiple_of(c * DSA_TK, DSA_TK)
        raw = _dot_nt(q, k_ref[0, 0, pl.ds(k0, DSA_TK), :])
        g = bias_ref[0, 0, c].astype(F32) - jnp.abs(pq - pk_ref[0, c]).astype(F32)
        u = (raw.reshape(DSA_GROUP, DSA_TQ, DSA_TK) + slope * g[None]).reshape(rq, DSA_TK)
        m_new = jnp.maximum(m, jnp.max(u, axis=-1, keepdims=True))
        alpha = jnp.exp2((m - m_new) * scale2)
        p = jnp.exp2((u - m_new) * scale2)
        l = alpha * l + jnp.sum(p, axis=-1, keepdims=True)
        acc = alpha * acc + jnp.dot(p.astype(BF16), v_ref[0, 0, pl.ds(k0, DSA_TK), :], preferred_element_type=F32)
        return m_new, l, acc

    init = (jnp.full((rq, 1), -jnp.inf, F32), jnp.zeros((rq, 1), F32), jnp.zeros((rq, HEAD), F32))
    _, l, acc = lax.fori_loop(0, (i * DSA_TQ) // DSA_TK + 1, step, init)
    o = acc / l
    for r in range(DSA_GROUP):
        o_ref[0, :, r * HEAD:(r + 1) * HEAD] = o[r * DSA_TQ:(r + 1) * DSA_TQ].astype(o_ref.dtype)


def dsa_attention(qkv_hm, bias, positions, slopes):
    b, _, s, _ = qkv_hm.shape
    nq, nc = s // DSA_TQ, s // DSA_TK
    kv = (1, 1, s, HEAD)
    scale = HEAD ** -0.5
    slope_rows = jnp.repeat((slopes / scale).reshape(DSA_KV_HEADS, DSA_GROUP), DSA_TQ, axis=1)[..., None]
    return pl.pallas_call(
        functools.partial(_dsa_attn_kernel, scale2=scale * LOG2E),
        grid=(b, DSA_KV_HEADS, nq),
        in_specs=[pl.BlockSpec((1, DSA_GROUP, DSA_TQ, HEAD), lambda bi, g, i: (bi, g, i, 0)),
                  pl.BlockSpec(kv, lambda bi, g, i: (bi, N_HEADS + g, 0, 0)),
                  pl.BlockSpec(kv, lambda bi, g, i: (bi, N_HEADS + DSA_KV_HEADS + g, 0, 0)),
                  pl.BlockSpec((1, 1, nc, DSA_TQ, DSA_TK), lambda bi, g, i: (bi, i, 0, 0, 0)),
                  pl.BlockSpec((1, DSA_TQ, 1), lambda bi, g, i: (bi, i, 0)),
                  pl.BlockSpec((1, nc, 1, DSA_TK), lambda bi, g, i: (bi, 0, 0, 0)),
                  pl.BlockSpec((1, DSA_GROUP * DSA_TQ, 1), lambda bi, g, i: (g, 0, 0))],
        out_specs=pl.BlockSpec((1, DSA_TQ, DSA_GROUP * HEAD), lambda bi, g, i: (bi, i, g)),
        out_shape=jax.ShapeDtypeStruct((b, s, N_HEADS * HEAD), BF16),
        compiler_params=_params(("parallel", "parallel", "arbitrary"), 32 << 20),
        name="dsa_attention",
    )(qkv_hm, qkv_hm, qkv_hm, bias, positions.reshape(b, s, 1), positions.reshape(b, nc, 1, DSA_TK),
      slope_rows)


def _rotate_half_cols(r):
    half = r.shape[-1] // 2
    return jnp.concatenate([-r[..., half:], r[..., :half]], axis=-1)


def _mla_layer(n, h, cs, b, j, w_a, q_norm, kv_norm, w_uq, w_ukv, w_o):
    t, d = n.shape
    lat = MLA_Q_LORA + MLA_KV_LORA
    kr_w = w_a[j, :, lat:lat + MLA_ROPE]
    w_a2 = jnp.concatenate([w_a[j, :, :lat], kr_w, _rotate_half_cols(kr_w), jnp.zeros((d, LANES), F32)], axis=1)
    a = proj(n, w_a2, w_a2.shape[1], tm=1024, tn=256, out_dtype=F32)
    cq, ckv, kr = mla_prep(a, cs, q_norm[j], kv_norm[j])
    w3 = w_uq[j].reshape(MLA_Q_LORA, N_HEADS, MLA_NOPE + MLA_ROPE)
    rope_w = w3[:, :, MLA_NOPE:]
    w_uq2 = jnp.concatenate([w3[:, :, :MLA_NOPE].reshape(MLA_Q_LORA, -1),
                             jnp.concatenate([rope_w, _rotate_half_cols(rope_w)], -1).reshape(MLA_Q_LORA, -1)], 1)
    q_hm = proj(cq, w_uq2, w_uq2.shape[1], tm=1024, tn=512, out_dtype=BF16, heads_batch=b)
    kv_hm = proj(ckv, w_ukv, w_ukv.shape[2], layer=j, tm=1024, tn=512, out_dtype=BF16, heads_batch=b)
    s = t // b
    o = mla_attention(q_hm, kv_hm, kr.reshape(b, s, LANES), cs.reshape(b, s, LANES))
    return proj(o.reshape(t, d), w_o, d, layer=j, tm=1024, tn=512, out_dtype=F32, res=h)


def _sb_layer(n, h, b, j, w_qkv, w_o):
    t, d = n.shape
    qkv_hm = proj(n, w_qkv, w_qkv.shape[2], layer=j, tm=1024, tn=512, out_dtype=BF16, heads_batch=b)
    o = sb_attention(qkv_hm)
    return proj(o.reshape(t, d), w_o, d, layer=j, tm=1024, tn=512, out_dtype=F32, res=h)


def _dsa_layer(n, h, positions, b, j, w_in, idx_k_norm, w_o):
    t, d = n.shape
    s = t // b
    n_qkv = (N_HEADS + 2 * DSA_KV_HEADS) * HEAD
    n_iq = N_HEADS * HEAD
    qkv_hm = proj(n, w_in, n_qkv, layer=j, tm=1024, tn=512, out_dtype=BF16, heads_batch=b)
    iq_hm = proj(n, w_in, n_iq, layer=j, col_off=n_qkv, tm=1024, tn=512, out_dtype=BF16, heads_batch=b)
    w_tail = jnp.pad(w_in[j, :, n_qkv + n_iq:], ((0, 0), (0, 2 * LANES - (HEAD + N_HEADS))))
    tail = proj(n, w_tail, 2 * LANES, tm=1024, tn=256, out_dtype=F32)
    bias = dsa_index(iq_hm, tail.reshape(b, s, 2 * LANES), idx_k_norm[j], min(DSA_TOPK_MAX, s // 4))
    slopes = jnp.exp2(-8.0 * jnp.arange(1, N_HEADS + 1, dtype=F32) / N_HEADS)
    o = dsa_attention(qkv_hm, bias, positions, slopes)
    return proj(o.reshape(t, d), w_o, d, layer=j, tm=1024, tn=512, out_dtype=F32, res=h)


def _rope_table(positions):
    inv = ROPE_THETA ** (-jnp.arange(0, MLA_ROPE, 2, dtype=F32) / MLA_ROPE)
    ang = positions.astype(F32)[..., None] * inv
    cos, sin = jnp.cos(ang), jnp.sin(ang)
    return jnp.concatenate([cos, cos, sin, sin], axis=-1)


@jax.jit
def kernel(x, positions, attn_norm, ffn_norm, final_norm, ffn_w_up, ffn_conv_w, ffn_conv_b, ffn_w_down,
           mla_w_a, mla_q_norm, mla_kv_norm, mla_w_uq, mla_w_ukv, mla_w_o, sb_w_qkv, sb_w_o,
           dsa_w_in, dsa_idx_k_norm, dsa_w_o):
    b, s, d = x.shape
    t = b * s
    h = x.reshape(t, d)
    cs = _rope_table(positions).reshape(t, LANES)
    for i in range(DEPTH):
        kind, j = i % N_MIXERS, i // N_MIXERS
        n = rmsnorm(h, attn_norm[i], BF16)
        if kind == 0:
            h = _mla_layer(n, h, cs, b, j, mla_w_a, mla_q_norm, mla_kv_norm, mla_w_uq, mla_w_ukv, mla_w_o)
        elif kind == 1:
            h = _sb_layer(n, h, b, j, sb_w_qkv, sb_w_o)
        else:
            h = _dsa_layer(n, h, positions, b, j, dsa_w_in, dsa_idx_k_norm, dsa_w_o)
        n2 = rmsnorm(h, ffn_norm[i], BF16)
        act, w_down = ffn_up(n2, ffn_w_up, ffn_conv_w, ffn_conv_b, ffn_w_down, i, s)
        h = down_proj(act, w_down, h)
    return rmsnorm(h, final_norm, F32).reshape(b, s, d)
```

```python
import functools

import jax
import jax.numpy as jnp
from jax import lax
from jax.experimental import pallas as pl
from jax.experimental.pallas import tpu as pltpu

F32 = jnp.float32
BF16 = jnp.bfloat16

DEPTH = 4
N_MIXERS = 3
EPS = 1e-6
HEAD = 128
N_HEADS = 32
MLA_Q_LORA = 1024
MLA_KV_LORA = 512
MLA_NOPE = 128
MLA_ROPE = 64
ROPE_THETA = 10000.0
DSA_KV_HEADS = 8
DSA_GROUP = N_HEADS // DSA_KV_HEADS
DSA_TOPK_MAX = 256
FFN_DIM = 11008
CONV_WIDTH = 3

LANES = 128
SUBLANES = 8
VMEM_CAP = 56 * 1024 * 1024
NEG_BIG = -1e30
INT_MIN = -2 ** 31


def _params(sem, vmem_bytes):
    return pltpu.CompilerParams(dimension_semantics=sem,
                                vmem_limit_bytes=int(min(VMEM_CAP, max(vmem_bytes, 16 * 1024 * 1024))))


def _dot_nt(a, b):
    return lax.dot_general(a, b, (((1,), (1,)), ((), ())), preferred_element_type=F32)


def _rmsnorm_kernel(x_ref, g_ref, o_ref):
    x = x_ref[...]
    y = x * lax.rsqrt(jnp.mean(x * x, axis=-1, keepdims=True) + EPS)
    o_ref[...] = (y * g_ref[...]).astype(o_ref.dtype)


def rmsnorm(x, g, out_dtype, tm=256):
    t, d = x.shape
    return pl.pallas_call(
        _rmsnorm_kernel,
        grid=(t // tm,),
        in_specs=[pl.BlockSpec((tm, d), lambda i: (i, 0)), pl.BlockSpec((1, d), lambda i: (0, 0))],
        out_specs=pl.BlockSpec((tm, d), lambda i: (i, 0)),
        out_shape=jax.ShapeDtypeStruct((t, d), out_dtype),
        compiler_params=_params(("parallel",), 6 * tm * d * 4),
        name="rmsnorm",
    )(x, g.reshape(1, d))


def _proj_kernel(*refs, has_res, heads_out):
    if has_res:
        x_ref, w_ref, r_ref, o_ref, wb_ref = refs
    else:
        x_ref, w_ref, o_ref, wb_ref = refs

    @pl.when(pl.program_id(1) == 0)
    def _():
        wb_ref[...] = w_ref[...].astype(BF16)

    acc = jnp.dot(x_ref[...], wb_ref[...], preferred_element_type=F32)
    if has_res:
        acc = r_ref[...] + acc
    if heads_out:
        for c in range(acc.shape[1] // HEAD):
            o_ref[0, c] = acc[:, c * HEAD:(c + 1) * HEAD].astype(o_ref.dtype)
    else:
        o_ref[...] = acc.astype(o_ref.dtype)


def proj(x, w, n_cols, *, layer=None, col_off=0, tm, tn, out_dtype, res=None, heads_batch=None):
    t, k = x.shape
    assert w.shape[-2] == k and n_cols % tn == 0 and t % tm == 0 and col_off % tn == 0
    assert (layer is None) == (w.ndim == 2)
    off_b = col_off // tn
    grid = (n_cols // tn, t // tm)
    if layer is None:
        w_spec = pl.BlockSpec((k, tn), lambda j, i: (0, j + off_b))
    else:
        w_spec = pl.BlockSpec((None, k, tn), lambda j, i: (layer, 0, j + off_b))
    in_specs = [pl.BlockSpec((tm, k), lambda j, i: (i, 0)), w_spec]
    args = [x, w]
    if res is not None:
        in_specs.append(pl.BlockSpec((tm, tn), lambda j, i: (i, j)))
        args.append(res)
    if heads_batch is None:
        out_shape = jax.ShapeDtypeStruct((t, n_cols), out_dtype)
        out_spec = pl.BlockSpec((tm, tn), lambda j, i: (i, j))
    else:
        s = t // heads_batch
        assert s % tm == 0
        nsb = s // tm
        out_shape = jax.ShapeDtypeStruct((heads_batch, n_cols // HEAD, s, HEAD), out_dtype)
        out_spec = pl.BlockSpec((1, tn // HEAD, tm, HEAD), lambda j, i: (i // nsb, j, i % nsb, 0))
    vmem = 2 * k * tn * 4 + k * tn * 2 + 2 * tm * k * 2 + 6 * tm * tn * 4
    return pl.pallas_call(
        functools.partial(_proj_kernel, has_res=res is not None, heads_out=heads_batch is not None),
        grid=grid,
        in_specs=in_specs,
        out_specs=out_spec,
        out_shape=out_shape,
        scratch_shapes=[pltpu.VMEM((k, tn), BF16)],
        compiler_params=_params(("arbitrary", "arbitrary"), vmem + (4 << 20)),
        name="proj",
    )(*args)


def _down_kernel(x_ref, w_ref, r_ref, o_ref):
    o_ref[...] = r_ref[...] + jnp.dot(x_ref[...], w_ref[...], preferred_element_type=F32)


def down_proj(x, w_bf16, res, *, tm=512, tn=512):
    t, k = x.shape
    n = w_bf16.shape[1]
    vmem = 2 * tm * k * 2 + 2 * k * tn * 2 + 6 * tm * tn * 4
    return pl.pallas_call(
        _down_kernel,
        grid=(t // tm, n // tn),
        in_specs=[pl.BlockSpec((tm, k), lambda i, j: (i, 0)),
                  pl.BlockSpec((k, tn), lambda i, j: (0, j)),
                  pl.BlockSpec((tm, tn), lambda i, j: (i, j))],
        out_specs=pl.BlockSpec((tm, tn), lambda i, j: (i, j)),
        out_shape=jax.ShapeDtypeStruct((t, n), F32),
        compiler_params=_params(("parallel", "parallel"), vmem + (4 << 20)),
        name="down_proj",
    )(x, w_bf16, res)


def _shift_rows(u, prev, k):
    rolled = pltpu.roll(u, k, axis=0)
    top = jnp.where(lax.broadcasted_iota(jnp.int32, (SUBLANES, u.shape[1]), 0) < k,
                    pltpu.roll(prev, k, axis=0), rolled[:SUBLANES])
    return jnp.concatenate([top, rolled[SUBLANES:]], axis=0)


def _ffn_up_kernel(x_ref, wg_ref, wv_ref, cwg_ref, cwv_ref, cbg_ref, cbv_ref, wd_ref, o_ref, wdb_ref,
                   wgb_ref, wvb_ref, pg_ref, pv_ref, *, tiles_per_seq, chunk):
    i = pl.program_id(1)

    wdb_ref[...] = wd_ref[...].astype(BF16)

    @pl.when(i == 0)
    def _():
        wgb_ref[...] = wg_ref[...].astype(BF16)
        wvb_ref[...] = wv_ref[...].astype(BF16)

    @pl.when(i % tiles_per_seq == 0)
    def _():
        pg_ref[0:SUBLANES, :] = jnp.zeros((SUBLANES, pg_ref.shape[1]), F32)
        pv_ref[0:SUBLANES, :] = jnp.zeros((SUBLANES, pv_ref.shape[1]), F32)

    def conv(u_ref, cw_ref, cb_ref):
        c = cb_ref[...] + cw_ref[0:1, :] * u_ref[SUBLANES - 2:SUBLANES - 2 + chunk, :]
        c = c + cw_ref[1:2, :] * u_ref[SUBLANES - 1:SUBLANES - 1 + chunk, :]
        return c + cw_ref[2:3, :] * u_ref[SUBLANES:SUBLANES + chunk, :]

    for r in range(0, x_ref.shape[0], chunk):
        x = x_ref[r:r + chunk, :]
        pg_ref[SUBLANES:, :] = jnp.dot(x, wgb_ref[...], preferred_element_type=F32)
        pv_ref[SUBLANES:, :] = jnp.dot(x, wvb_ref[...], preferred_element_type=F32)
        gate = conv(pg_ref, cwg_ref, cbg_ref)
        val = conv(pv_ref, cwv_ref, cbv_ref)
        o_ref[r:r + chunk, :] = (gate * jax.nn.sigmoid(gate) * val).astype(o_ref.dtype)
        pg_ref[0:SUBLANES, :] = pg_ref[chunk:chunk + SUBLANES, :]
        pv_ref[0:SUBLANES, :] = pv_ref[chunk:chunk + SUBLANES, :]


def ffn_up(x, w_up, conv_w, conv_b, w_down, layer, seq_len, *, tm=1024, tn=256, chunk=128):
    t, k = x.shape
    f = w_up.shape[2] // 2
    nfb, nmt = f // tn, t // tm
    assert f % tn == 0 and seq_len % tm == 0 and t % seq_len == 0 and tm % chunk == 0
    kd, nd = w_down.shape[1:]
    slab = kd // (nfb * nmt)
    assert slab * nfb * nmt == kd and slab % (2 * SUBLANES) == 0
    cb = conv_b.reshape(conv_b.shape[0], 1, 2 * f)
    vmem = 4 * k * tn * 4 + 2 * k * tn * 2 + 2 * tm * k * 2 + 12 * tm * tn * 4 + 12 * slab * nd
    return pl.pallas_call(
        functools.partial(_ffn_up_kernel, tiles_per_seq=seq_len // tm, chunk=chunk),
        grid=(nfb, nmt),
        in_specs=[pl.BlockSpec((tm, k), lambda j, i: (i, 0)),
                  pl.BlockSpec((None, k, tn), lambda j, i: (layer, 0, j)),
                  pl.BlockSpec((None, k, tn), lambda j, i: (layer, 0, j + nfb)),
                  pl.BlockSpec((None, CONV_WIDTH, tn), lambda j, i: (layer, 0, j)),
                  pl.BlockSpec((None, CONV_WIDTH, tn), lambda j, i: (layer, 0, j + nfb)),
                  pl.BlockSpec((None, 1, tn), lambda j, i: (layer, 0, j)),
                  pl.BlockSpec((None, 1, tn), lambda j, i: (layer, 0, j + nfb)),
                  pl.BlockSpec((None, slab, nd), lambda j, i: (layer, j * nmt + i, 0))],
        out_specs=[pl.BlockSpec((tm, tn), lambda j, i: (i, j)),
                   pl.BlockSpec((slab, nd), lambda j, i: (j * nmt + i, 0))],
        out_shape=[jax.ShapeDtypeStruct((t, f), BF16), jax.ShapeDtypeStruct((kd, nd), BF16)],
        scratch_shapes=[pltpu.VMEM((k, tn), BF16), pltpu.VMEM((k, tn), BF16),
                        pltpu.VMEM((SUBLANES + chunk, tn), F32), pltpu.VMEM((SUBLANES + chunk, tn), F32)],
        compiler_params=_params(("arbitrary", "arbitrary"), vmem + (4 << 20)),
        name="ffn_up",
    )(x, w_up, w_up, conv_w, conv_w, cb, cb, w_down)


def _mla_prep_kernel(a_ref, cs_ref, gq_ref, gkv_ref, cq_ref, ckv_ref, kr_ref):
    def norm(v, g):
        return v * lax.rsqrt(jnp.mean(v * v, axis=-1, keepdims=True) + EPS) * g

    cq_ref[...] = norm(a_ref[:, 0:MLA_Q_LORA], gq_ref[...]).astype(cq_ref.dtype)
    ckv_ref[...] = norm(a_ref[:, MLA_Q_LORA:MLA_Q_LORA + MLA_KV_LORA], gkv_ref[...]).astype(ckv_ref.dtype)
    off = MLA_Q_LORA + MLA_KV_LORA
    y = a_ref[:, off:off + LANES] * cs_ref[...]
    kr_ref[...] = (y + pltpu.roll(y, MLA_ROPE, axis=1)).astype(kr_ref.dtype)


def mla_prep(a, cs, gq, gkv, tm=512):
    t, na = a.shape
    return pl.pallas_call(
        _mla_prep_kernel,
        grid=(t // tm,),
        in_specs=[pl.BlockSpec((tm, na), lambda i: (i, 0)),
                  pl.BlockSpec((tm, LANES), lambda i: (i, 0)),
                  pl.BlockSpec((1, MLA_Q_LORA), lambda i: (0, 0)),
                  pl.BlockSpec((1, MLA_KV_LORA), lambda i: (0, 0))],
        out_specs=[pl.BlockSpec((tm, MLA_Q_LORA), lambda i: (i, 0)),
                   pl.BlockSpec((tm, MLA_KV_LORA), lambda i: (i, 0)),
                   pl.BlockSpec((tm, LANES), lambda i: (i, 0))],
        out_shape=[jax.ShapeDtypeStruct((t, MLA_Q_LORA), BF16),
                   jax.ShapeDtypeStruct((t, MLA_KV_LORA), BF16),
                   jax.ShapeDtypeStruct((t, LANES), BF16)],
        compiler_params=_params(("parallel",), 6 * tm * na * 4),
        name="mla_prep",
    )(a, cs, gq.reshape(1, -1), gkv.reshape(1, -1))


def _softmax_step(s2, m, acc, v_ones):
    m_new = jnp.maximum(m, jnp.max(s2, axis=-1, keepdims=True))
    p = jnp.exp2(s2 - m_new).astype(BF16)
    acc = jnp.exp2(m - m_new) * acc + jnp.dot(p, v_ones, preferred_element_type=F32)
    return m_new, acc


def _softmax_init(rows):
    return jnp.full((rows, 1), -jnp.inf, F32), jnp.zeros((rows, 2 * HEAD), F32)


def _softmax_finish(acc):
    return acc[:, :HEAD] / acc[:, HEAD:]


def _mla_attn_kernel(qn_ref, qr_ref, kn_ref, v_ref, kr_ref, cs_ref, o_ref, qcat_ref, kcat_ref, vone_ref,
                     *, seq, tq, tk, scale2):
    qcat_ref[:, 0:HEAD] = qn_ref[0, 0]
    qcat_ref[:, HEAD:2 * HEAD] = (qr_ref[0, 0].astype(F32) * cs_ref[0]).astype(BF16)
    kcat_ref[:, 0:HEAD] = kn_ref[0, 0]
    kcat_ref[:, HEAD:2 * HEAD] = kr_ref[0]
    vone_ref[:, 0:HEAD] = v_ref[0, 0]
    vone_ref[:, HEAD:2 * HEAD] = jnp.ones((seq, HEAD), BF16)

    def q_block(i, _):
        r0 = pl.multiple_of(i * tq, tq)
        q = qcat_ref[pl.ds(r0, tq), :]

        def scores(c):
            return _dot_nt(q, kcat_ref[pl.ds(pl.multiple_of(c * tk, tk), tk), :]) * scale2

        def values(c):
            return vone_ref[pl.ds(pl.multiple_of(c * tk, tk), tk), :]

        def body(c, carry):
            m, acc, s2 = carry
            s2_next = scores(c + 1)
            m, acc = _softmax_step(s2, m, acc, values(c))
            return m, acc, s2_next

        n_full = (i * tq) // tk
        m, acc, s2 = lax.fori_loop(0, n_full, body, _softmax_init(tq) + (scores(0),))
        rows = r0 + lax.broadcasted_iota(jnp.int32, (tq, tk), 0)
        cols = n_full * tk + lax.broadcasted_iota(jnp.int32, (tq, tk), 1)
        _, acc = _softmax_step(jnp.where(cols <= rows, s2, -jnp.inf), m, acc, values(n_full))
        o_ref[0, pl.ds(r0, tq), :] = _softmax_finish(acc).astype(o_ref.dtype)
        return 0

    lax.fori_loop(0, seq // tq, q_block, 0)


LOG2E = 1.4426950408889634


def mla_attention(q_hm, kv_hm, kr, cs, *, tq=512, tk=1024):
    b, _, s, _ = q_hm.shape
    tk = min(tk, s)
    assert tk % tq == 0 and s % tk == 0
    blk = (1, 1, s, HEAD)
    return pl.pallas_call(
        functools.partial(_mla_attn_kernel, seq=s, tq=tq, tk=tk, scale2=(MLA_NOPE + MLA_ROPE) ** -0.5 * LOG2E),
        grid=(b, N_HEADS),
        in_specs=[pl.BlockSpec(blk, lambda bi, h: (bi, h, 0, 0)),
                  pl.BlockSpec(blk, lambda bi, h: (bi, N_HEADS + h, 0, 0)),
                  pl.BlockSpec(blk, lambda bi, h: (bi, 2 * h, 0, 0)),
                  pl.BlockSpec(blk, lambda bi, h: (bi, 2 * h + 1, 0, 0)),
                  pl.BlockSpec((1, s, HEAD), lambda bi, h: (bi, 0, 0)),
                  pl.BlockSpec((1, s, HEAD), lambda bi, h: (bi, 0, 0))],
        out_specs=pl.BlockSpec((1, s, HEAD), lambda bi, h: (bi, 0, h)),
        out_shape=jax.ShapeDtypeStruct((b, s, N_HEADS * HEAD), BF16),
        scratch_shapes=[pltpu.VMEM((s, 2 * HEAD), BF16)] * 3,
        compiler_params=_params(("parallel", "parallel"), 40 << 20),
        name="mla_attention",
    )(q_hm, q_hm, kv_hm, kv_hm, kr, cs)


SB_EXIT_LOG = -104.0


def _sb_attn_kernel(q_ref, k_ref, v_ref, u_ref, o_ref, *, seq, t, scale, heads):
    rows = lax.broadcasted_iota(jnp.int32, (t, t), 0)
    cols = lax.broadcasted_iota(jnp.int32, (t, t), 1)
    past = cols < rows

    def tile(h, q, j, c, diag):
        k0 = pl.multiple_of(j * t, t)
        z = _dot_nt(q, k_ref[0, h, pl.ds(k0, t), :]) * scale
        log_beta = jnp.minimum(z, 0.0) - jnp.log(1.0 + jnp.exp(-jnp.abs(z)))
        log_keep = log_beta - z
        if diag:
            log_keep = jnp.where(past, log_keep, 0.0)
        hi = log_keep.astype(BF16)
        lo = (log_keep - hi.astype(F32)).astype(BF16)
        tri = u_ref[...]
        later = jnp.dot(hi, tri, preferred_element_type=F32) + jnp.dot(lo, tri, preferred_element_type=F32)
        a = jnp.exp(log_beta + later + c)
        if diag:
            a = jnp.where(past, a, 0.0)
        pv = jnp.dot(a.astype(BF16), v_ref[0, h, pl.ds(k0, t), :], preferred_element_type=F32)
        return pv, jnp.sum(log_keep, axis=-1, keepdims=True)

    def q_block(i, _):
        r0 = pl.multiple_of(i * t, t)
        qs = [q_ref[0, h, pl.ds(r0, t), :] for h in range(heads)]

        def step(j, cs, accs, diag):
            outs = [tile(h, qs[h], j, cs[h], diag) for h in range(heads)]
            return (tuple(c + o[1] for c, o in zip(cs, outs)), tuple(a + o[0] for a, o in zip(accs, outs)))

        def any_live(cs):
            top = cs[0]
            for c in cs[1:]:
                top = jnp.maximum(top, c)
            return (jnp.max(top) > SB_EXIT_LOG).astype(jnp.int32)

        zeros = lambda w: tuple(jnp.zeros((t, w), F32) for _ in range(heads))
        cs, accs = step(i, zeros(1), zeros(HEAD), True)

        def cond(carry):
            return jnp.logical_and(carry[0] >= 0, carry[3] > 0)

        def body(carry):
            j, cs, accs, _ = carry
            cs, accs = step(j, cs, accs, False)
            return j - 1, cs, accs, any_live(cs)

        _, _, accs, _ = lax.while_loop(cond, body, (i - 1, cs, accs, any_live(cs)))
        for h in range(heads):
            o_ref[0, pl.ds(r0, t), h * HEAD:(h + 1) * HEAD] = accs[h].astype(o_ref.dtype)
        return 0

    lax.fori_loop(0, seq // t, q_block, 0)


def sb_attention(qkv_hm, *, t=256, heads=4):
    b, _, s, _ = qkv_hm.shape
    blk = (1, heads, s, HEAD)
    nhb = N_HEADS // heads
    tri = (jnp.arange(t)[:, None] > jnp.arange(t)[None, :]).astype(BF16)
    return pl.pallas_call(
        functools.partial(_sb_attn_kernel, seq=s, t=t, scale=HEAD ** -0.5, heads=heads),
        grid=(b, nhb),
        in_specs=[pl.BlockSpec(blk, lambda bi, h: (bi, h, 0, 0)),
                  pl.BlockSpec(blk, lambda bi, h: (bi, nhb + h, 0, 0)),
                  pl.BlockSpec(blk, lambda bi, h: (bi, 2 * nhb + h, 0, 0)),
                  pl.BlockSpec((t, t), lambda bi, h: (0, 0))],
        out_specs=pl.BlockSpec((1, s, heads * HEAD), lambda bi, h: (bi, 0, h)),
        out_shape=jax.ShapeDtypeStruct((b, s, N_HEADS * HEAD), BF16),
        compiler_params=_params(("parallel", "parallel"), 40 << 20),
        name="sb_attention",
    )(qkv_hm, qkv_hm, qkv_hm, tri)


DSA_TQ = 256
DSA_TK = 1024


def _dsa_index_kernel(iq_ref, tail_ref, iw_ref, g_ref, bias_ref, ikn_ref, key_ref,
                      *, n_chunks, top_k, w_scale, heads_per_dot):
    i = pl.program_id(1)

    @pl.when(i == 0)
    def _():
        ik = tail_ref[0]
        y = ik * lax.rsqrt(jnp.mean(ik * ik, axis=-1, keepdims=True) + EPS)
        ikn_ref[...] = (y * g_ref[...]).astype(BF16)

    n_live = (i * DSA_TQ) // DSA_TK + 1
    w = iw_ref[0][:, 0:N_HEADS] * w_scale
    rows = i * DSA_TQ + lax.broadcasted_iota(jnp.int32, (DSA_TQ, DSA_TK), 0)
    cols0 = lax.broadcasted_iota(jnp.int32, (DSA_TQ, DSA_TK), 1)

    def score_chunk(c, _):
        k0 = pl.multiple_of(c * DSA_TK, DSA_TK)
        ik = ikn_ref[pl.ds(k0, DSA_TK), :]
        sc = jnp.zeros((DSA_TQ, DSA_TK), F32)
        for hg in range(N_HEADS // heads_per_dot):
            q = iq_ref[0, hg * heads_per_dot:(hg + 1) * heads_per_dot].reshape(heads_per_dot * DSA_TQ, HEAD)
            d = _dot_nt(q, ik)
            for hh in range(heads_per_dot):
                h = hg * heads_per_dot + hh
                sc = sc + w[:, h:h + 1] * jnp.maximum(d[hh * DSA_TQ:(hh + 1) * DSA_TQ], 0.0)
        bits = pltpu.bitcast(sc, jnp.int32)
        key = jnp.where(bits < 0, bits ^ jnp.int32(0x7FFFFFFF), bits)
        key_ref[c] = jnp.where(k0 + cols0 <= rows, key, jnp.int32(INT_MIN))
        return 0

    lax.fori_loop(0, n_live, score_chunk, 0)

    def fill_dead(c, _):
        key_ref[c] = jnp.full((DSA_TQ, DSA_TK), INT_MIN, jnp.int32)
        return 0

    def bisect(n_counted):
        lax.fori_loop(n_live, n_counted, fill_dead, 0)

        def bit_step(n, thr):
            cand = thr + lax.shift_left(jnp.int32(1), jnp.int32(31) - n)
            cnt = jnp.zeros((DSA_TQ, DSA_TK), jnp.int32)
            for c in range(n_counted):
                cnt = cnt + (key_ref[c] >= cand).astype(jnp.int32)
            return jnp.where(jnp.sum(cnt, axis=-1, keepdims=True) >= top_k, cand, thr)

        return lax.fori_loop(0, 32, bit_step, jnp.full((DSA_TQ, 1), INT_MIN, jnp.int32))

    half = n_chunks // 2
    if half >= 1:
        thr = lax.cond(n_live <= half, lambda: bisect(half), lambda: bisect(n_chunks))
    else:
        thr = bisect(n_chunks)
    thr = jnp.maximum(thr, jnp.int32(INT_MIN + 1))

    def write_live(c, _):
        bias_ref[0, 0, c] = jnp.where(key_ref[c] >= thr, 0.0, NEG_BIG).astype(bias_ref.dtype)
        return 0

    def write_dead(c, _):
        bias_ref[0, 0, c] = jnp.full((DSA_TQ, DSA_TK), NEG_BIG, bias_ref.dtype)
        return 0

    lax.fori_loop(0, n_live, write_live, 0)
    lax.fori_loop(n_live, n_chunks, write_dead, 0)


def dsa_index(iq_hm, tail, g_ik, top_k):
    b, _, s, _ = iq_hm.shape
    nq, nc = s // DSA_TQ, s // DSA_TK
    w_scale = N_HEADS ** -0.5 * HEAD ** -0.5
    return pl.pallas_call(
        functools.partial(_dsa_index_kernel, n_chunks=nc, top_k=top_k, w_scale=w_scale, heads_per_dot=8),
        grid=(b, nq),
        in_specs=[pl.BlockSpec((1, N_HEADS, DSA_TQ, HEAD), lambda bi, i: (bi, 0, i, 0)),
                  pl.BlockSpec((1, s, HEAD), lambda bi, i: (bi, 0, 0)),
                  pl.BlockSpec((1, DSA_TQ, HEAD), lambda bi, i: (bi, i, 1)),
                  pl.BlockSpec((1, HEAD), lambda bi, i: (0, 0))],
        out_specs=pl.BlockSpec((1, 1, nc, DSA_TQ, DSA_TK), lambda bi, i: (bi, i, 0, 0, 0)),
        out_shape=jax.ShapeDtypeStruct((b, nq, nc, DSA_TQ, DSA_TK), BF16),
        scratch_shapes=[pltpu.VMEM((s, HEAD), BF16), pltpu.VMEM((nc, DSA_TQ, DSA_TK), jnp.int32)],
        compiler_params=_params(("arbitrary", "arbitrary"), 40 << 20),
        name="dsa_index",
    )(iq_hm, tail, tail, g_ik.reshape(1, HEAD))


def _dsa_attn_kernel(q_ref, k_ref, v_ref, bias_ref, pq_ref, pk_ref, slope_ref, o_ref, *, scale2):
    i = pl.program_id(2)
    rq = DSA_GROUP * DSA_TQ
    q = q_ref[0].reshape(rq, HEAD)
    slope = slope_ref[0].reshape(DSA_GROUP, DSA_TQ, 1)
    pq = pq_ref[0]

    def step(c, carry):
        m, l, acc = carry
        k0 = pl.multiple_of(c * DSA_TK, DSA_TK)
        raw = _dot_nt(q, k_ref[0, 0, pl.ds(k0, DSA_TK), :])
        g = bias_ref[0, 0, c].astype(F32) - jnp.abs(pq - pk_ref[0, c]).astype(F32)
        u = (raw.reshape(DSA_GROUP, DSA_TQ, DSA_TK) + slope * g[None]).reshape(rq, DSA_TK)
        m_new = jnp.maximum(m, jnp.max(u, axis=-1, keepdims=True))
        alpha = jnp.exp2((m - m_new) * scale2)
        p = jnp.exp2((u - m_new) * scale2)
        l = alpha * l + jnp.sum(p, axis=-1, keepdims=True)
        acc = alpha * acc + jnp.dot(p.astype(BF16), v_ref[0, 0, pl.ds(k0, DSA_TK), :], preferred_element_type=F32)
        return m_new, l, acc

    init = (jnp.full((rq, 1), -jnp.inf, F32), jnp.zeros((rq, 1), F32), jnp.zeros((rq, HEAD), F32))
    _, l, acc = lax.fori_loop(0, (i * DSA_TQ) // DSA_TK + 1, step, init)
    o = acc / l
    for r in range(DSA_GROUP):
        o_ref[0, :, r * HEAD:(r + 1) * HEAD] = o[r * DSA_TQ:(r + 1) * DSA_TQ].astype(o_ref.dtype)


def dsa_attention(qkv_hm, bias, positions, slopes):
    b, _, s, _ = qkv_hm.shape
    nq, nc = s // DSA_TQ, s // DSA_TK
    kv = (1, 1, s, HEAD)
    scale = HEAD ** -0.5
    slope_rows = jnp.repeat((slopes / scale).reshape(DSA_KV_HEADS, DSA_GROUP), DSA_TQ, axis=1)[..., None]
    return pl.pallas_call(
        functools.partial(_dsa_attn_kernel, scale2=scale * LOG2E),
        grid=(b, DSA_KV_HEADS, nq),
        in_specs=[pl.BlockSpec((1, DSA_GROUP, DSA_TQ, HEAD), lambda bi, g, i: (bi, g, i, 0)),
                  pl.BlockSpec(kv, lambda bi, g, i: (bi, N_HEADS + g, 0, 0)),
                  pl.BlockSpec(kv, lambda bi, g, i: (bi, N_HEADS + DSA_KV_HEADS + g, 0, 0)),
                  pl.BlockSpec((1, 1, nc, DSA_TQ, DSA_TK), lambda bi, g, i: (bi, i, 0, 0, 0)),
                  pl.BlockSpec((1, DSA_TQ, 1), lambda bi, g, i: (bi, i, 0)),
                  pl.BlockSpec((1, nc, 1, DSA_TK), lambda bi, g, i: (bi, 0, 0, 0)),
                  pl.BlockSpec((1, DSA_GROUP * DSA_TQ, 1), lambda bi, g, i: (g, 0, 0))],
        out_specs=pl.BlockSpec((1, DSA_TQ, DSA_GROUP * HEAD), lambda bi, g, i: (bi, i, g)),
        out_shape=jax.ShapeDtypeStruct((b, s, N_HEADS * HEAD), BF16),
        compiler_params=_params(("parallel", "parallel", "arbitrary"), 32 << 20),
        name="dsa_attention",
    )(qkv_hm, qkv_hm, qkv_hm, bias, positions.reshape(b, s, 1), positions.reshape(b, nc, 1, DSA_TK),
      slope_rows)


def _rotate_half_cols(r):
    half = r.shape[-1] // 2
    return jnp.concatenate([-r[..., half:], r[..., :half]], axis=-1)


def _mla_layer(n, h, cs, b, j, w_a, q_norm, kv_norm, w_uq, w_ukv, w_o):
    t, d = n.shape
    lat = MLA_Q_LORA + MLA_KV_LORA
    kr_w = w_a[j, :, lat:lat + MLA_ROPE]
    w_a2 = jnp.concatenate([w_a[j, :, :lat], kr_w, _rotate_half_cols(kr_w), jnp.zeros((d, LANES), F32)], axis=1)
    a = proj(n, w_a2, w_a2.shape[1], tm=1024, tn=256, out_dtype=F32)
    cq, ckv, kr = mla_prep(a, cs, q_norm[j], kv_norm[j])
    w3 = w_uq[j].reshape(MLA_Q_LORA, N_HEADS, MLA_NOPE + MLA_ROPE)
    rope_w = w3[:, :, MLA_NOPE:]
    w_uq2 = jnp.concatenate([w3[:, :, :MLA_NOPE].reshape(MLA_Q_LORA, -1),
                             jnp.concatenate([rope_w, _rotate_half_cols(rope_w)], -1).reshape(MLA_Q_LORA, -1)], 1)
    q_hm = proj(cq, w_uq2, w_uq2.shape[1], tm=1024, tn=512, out_dtype=BF16, heads_batch=b)
    kv_hm = proj(ckv, w_ukv, w_ukv.shape[2], layer=j, tm=1024, tn=512, out_dtype=BF16, heads_batch=b)
    s = t // b
    o = mla_attention(q_hm, kv_hm, kr.reshape(b, s, LANES), cs.reshape(b, s, LANES))
    return proj(o.reshape(t, d), w_o, d, layer=j, tm=1024, tn=512, out_dtype=F32, res=h)


def _sb_layer(n, h, b, j, w_qkv, w_o):
    t, d = n.shape
    qkv_hm = proj(n, w_qkv, w_qkv.shape[2], layer=j, tm=1024, tn=512, out_dtype=BF16, heads_batch=b)
    o = sb_attention(qkv_hm)
    return proj(o.reshape(t, d), w_o, d, layer=j, tm=1024, tn=512, out_dtype=F32, res=h)


def _dsa_layer(n, h, positions, b, j, w_in, idx_k_norm, w_o):
    t, d = n.shape
    s = t // b
    n_qkv = (N_HEADS + 2 * DSA_KV_HEADS) * HEAD
    n_iq = N_HEADS * HEAD
    qkv_hm = proj(n, w_in, n_qkv, layer=j, tm=1024, tn=512, out_dtype=BF16, heads_batch=b)
    iq_hm = proj(n, w_in, n_iq, layer=j, col_off=n_qkv, tm=1024, tn=512, out_dtype=BF16, heads_batch=b)
    w_tail = jnp.pad(w_in[j, :, n_qkv + n_iq:], ((0, 0), (0, 2 * LANES - (HEAD + N_HEADS))))
    tail = proj(n, w_tail, 2 * LANES, tm=1024, tn=256, out_dtype=F32)
    bias = dsa_index(iq_hm, tail.reshape(b, s, 2 * LANES), idx_k_norm[j], min(DSA_TOPK_MAX, s // 4))
    slopes = jnp.exp2(-8.0 * jnp.arange(1, N_HEADS + 1, dtype=F32) / N_HEADS)
    o = dsa_attention(qkv_hm, bias, positions, slopes)
    return proj(o.reshape(t, d), w_o, d, layer=j, tm=1024, tn=512, out_dtype=F32, res=h)


def _rope_table(positions):
    inv = ROPE_THETA ** (-jnp.arange(0, MLA_ROPE, 2, dtype=F32) / MLA_ROPE)
    ang = positions.astype(F32)[..., None] * inv
    cos, sin = jnp.cos(ang), jnp.sin(ang)
    return jnp.concatenate([cos, cos, sin, sin], axis=-1)


@jax.jit
def kernel(x, positions, attn_norm, ffn_norm, final_norm, ffn_w_up, ffn_conv_w, ffn_conv_b, ffn_w_down,
           mla_w_a, mla_q_norm, mla_kv_norm, mla_w_uq, mla_w_ukv, mla_w_o, sb_w_qkv, sb_w_o,
           dsa_w_in, dsa_idx_k_norm, dsa_w_o):
    b, s, d = x.shape
    t = b * s
    h = x.reshape(t, d)
    cs = _rope_table(positions).reshape(t, LANES)
    for i in range(DEPTH):
        kind, j = i % N_MIXERS, i // N_MIXERS
        n = rmsnorm(h, attn_norm[i], BF16)
        if kind == 0:
            h = _mla_layer(n, h, cs, b, j, mla_w_a, mla_q_norm, mla_kv_norm, mla_w_uq, mla_w_ukv, mla_w_o)
        elif kind == 1:
            h = _sb_layer(n, h, b, j, sb_w_qkv, sb_w_o)
        else:
            h = _dsa_layer(n, h, positions, b, j, dsa_w_in, dsa_idx_k_norm, dsa_w_o)
        n2 = rmsnorm(h, ffn_norm[i], BF16)
        act, w_down = ffn_up(n2, ffn_w_up, ffn_conv_w, ffn_conv_b, ffn_w_down, i, s)
        h = down_proj(act, w_down, h)
    return rmsnorm(h, final_norm, F32).reshape(b, s, d)
```

```python
import functools

import jax
import jax.numpy as jnp
from jax import lax
from jax.experimental import pallas as pl
from jax.experimental.pallas import tpu as pltpu

F32 = jnp.float32
BF16 = jnp.bfloat16

DEPTH = 4
N_MIXERS = 3
EPS = 1e-6
HEAD = 128
N_HEADS = 32
MLA_Q_LORA = 1024
MLA_KV_LORA = 512
MLA_NOPE = 128
MLA_ROPE = 64
ROPE_THETA = 10000.0
DSA_KV_HEADS = 8
DSA_GROUP = N_HEADS // DSA_KV_HEADS
DSA_TOPK_MAX = 256
FFN_DIM = 11008
CONV_WIDTH = 3

LANES = 128
SUBLANES = 8
VMEM_CAP = 56 * 1024 * 1024
NEG_BIG = -1e30
INT_MIN = -2 ** 31


def _params(sem, vmem_bytes):
    return pltpu.CompilerParams(dimension_semantics=sem,
                                vmem_limit_bytes=int(min(VMEM_CAP, max(vmem_bytes, 16 * 1024 * 1024))))


def _dot_nt(a, b):
    return lax.dot_general(a, b, (((1,), (1,)), ((), ())), preferred_element_type=F32)


def _rmsnorm_kernel(x_ref, g_ref, o_ref):
    x = x_ref[...]
    y = x * lax.rsqrt(jnp.mean(x * x, axis=-1, keepdims=True) + EPS)
    o_ref[...] = (y * g_ref[...]).astype(o_ref.dtype)


def rmsnorm(x, g, out_dtype, tm=256):
    t, d = x.shape
    return pl.pallas_call(
        _rmsnorm_kernel,
        grid=(t // tm,),
        in_specs=[pl.BlockSpec((tm, d), lambda i: (i, 0)), pl.BlockSpec((1, d), lambda i: (0, 0))],
        out_specs=pl.BlockSpec((tm, d), lambda i: (i, 0)),
        out_shape=jax.ShapeDtypeStruct((t, d), out_dtype),
        compiler_params=_params(("parallel",), 6 * tm * d * 4),
        name="rmsnorm",
    )(x, g.reshape(1, d))


def _proj_kernel(*refs, has_res, heads_out):
    if has_res:
        x_ref, w_ref, r_ref, o_ref, wb_ref = refs
    else:
        x_ref, w_ref, o_ref, wb_ref = refs

    @pl.when(pl.program_id(1) == 0)
    def _():
        wb_ref[...] = w_ref[...].astype(BF16)

    acc = jnp.dot(x_ref[...], wb_ref[...], preferred_element_type=F32)
    if has_res:
        acc = r_ref[...] + acc
    if heads_out:
        for c in range(acc.shape[1] // HEAD):
            o_ref[0, c] = acc[:, c * HEAD:(c + 1) * HEAD].astype(o_ref.dtype)
    else:
        o_ref[...] = acc.astype(o_ref.dtype)


def proj(x, w, n_cols, *, layer=None, col_off=0, tm, tn, out_dtype, res=None, heads_batch=None):
    t, k = x.shape
    assert w.shape[-2] == k and n_cols % tn == 0 and t % tm == 0 and col_off % tn == 0
    assert (layer is None) == (w.ndim == 2)
    off_b = col_off // tn
    grid = (n_cols // tn, t // tm)
    if layer is None:
        w_spec = pl.BlockSpec((k, tn), lambda j, i: (0, j + off_b))
    else:
        w_spec = pl.BlockSpec((None, k, tn), lambda j, i: (layer, 0, j + off_b))
    in_specs = [pl.BlockSpec((tm, k), lambda j, i: (i, 0)), w_spec]
    args = [x, w]
    if res is not None:
        in_specs.append(pl.BlockSpec((tm, tn), lambda j, i: (i, j)))
        args.append(res)
    if heads_batch is None:
        out_shape = jax.ShapeDtypeStruct((t, n_cols), out_dtype)
        out_spec = pl.BlockSpec((tm, tn), lambda j, i: (i, j))
    else:
        s = t // heads_batch
        assert s % tm == 0
        nsb = s // tm
        out_shape = jax.ShapeDtypeStruct((heads_batch, n_cols // HEAD, s, HEAD), out_dtype)
        out_spec = pl.BlockSpec((1, tn // HEAD, tm, HEAD), lambda j, i: (i // nsb, j, i % nsb, 0))
    vmem = 2 * k * tn * 4 + k * tn * 2 + 2 * tm * k * 2 + 6 * tm * tn * 4
    return pl.pallas_call(
        functools.partial(_proj_kernel, has_res=res is not None, heads_out=heads_batch is not None),
        grid=grid,
        in_specs=in_specs,
        out_specs=out_spec,
        out_shape=out_shape,
        scratch_shapes=[pltpu.VMEM((k, tn), BF16)],
        compiler_params=_params(("arbitrary", "arbitrary"), vmem + (4 << 20)),
        name="proj",
    )(*args)


def _down_kernel(x_ref, w_ref, r_ref, o_ref):
    o_ref[...] = r_ref[...] + jnp.dot(x_ref[...], w_ref[...], preferred_element_type=F32)


def down_proj(x, w_bf16, res, *, tm=512, tn=512):
    t, k = x.shape
    n = w_bf16.shape[1]
    vmem = 2 * tm * k * 2 + 2 * k * tn * 2 + 6 * tm * tn * 4
    return pl.pallas_call(
        _down_kernel,
        grid=(t // tm, n // tn),
        in_specs=[pl.BlockSpec((tm, k), lambda i, j: (i, 0)),
                  pl.BlockSpec((k, tn), lambda i, j: (0, j)),
                  pl.BlockSpec((tm, tn), lambda i, j: (i, j))],
        out_specs=pl.BlockSpec((tm, tn), lambda i, j: (i, j)),
        out_shape=jax.ShapeDtypeStruct((t, n), F32),
        compiler_params=_params(("parallel", "parallel"), vmem + (4 << 20)),
        name="down_proj",
    )(x, w_bf16, res)


def _ffn_up_kernel(x_ref, wg_ref, wv_ref, cwg_ref, cwv_ref, cbg_ref, cbv_ref, wd_ref, o_ref, wdb_ref,
                   wgb_ref, wvb_ref, pg_ref, pv_ref, *, tiles_per_seq, chunk):
    i = pl.program_id(1)

    wdb_ref[...] = wd_ref[...].astype(BF16)

    @pl.when(i == 0)
    def _():
        wgb_ref[...] = wg_ref[...].astype(BF16)
        wvb_ref[...] = wv_ref[...].astype(BF16)

    @pl.when(i % tiles_per_seq == 0)
    def _():
        pg_ref[0:SUBLANES, :] = jnp.zeros((SUBLANES, pg_ref.shape[1]), F32)
        pv_ref[0:SUBLANES, :] = jnp.zeros((SUBLANES, pv_ref.shape[1]), F32)

    def conv(u_ref, cw_ref, cb_ref):
        c = cb_ref[...] + cw_ref[0:1, :] * u_ref[SUBLANES - 2:SUBLANES - 2 + chunk, :]
        c = c + cw_ref[1:2, :] * u_ref[SUBLANES - 1:SUBLANES - 1 + chunk, :]
        return c + cw_ref[2:3, :] * u_ref[SUBLANES:SUBLANES + chunk, :]

    for r in range(0, x_ref.shape[0], chunk):
        x = x_ref[r:r + chunk, :]
        pg_ref[SUBLANES:, :] = jnp.dot(x, wgb_ref[...], preferred_element_type=F32)
        pv_ref[SUBLANES:, :] = jnp.dot(x, wvb_ref[...], preferred_element_type=F32)
        gate = conv(pg_ref, cwg_ref, cbg_ref)
        val = conv(pv_ref, cwv_ref, cbv_ref)
        o_ref[r:r + chunk, :] = (gate * jax.nn.sigmoid(gate) * val).astype(o_ref.dtype)
        pg_ref[0:SUBLANES, :] = pg_ref[chunk:chunk + SUBLANES, :]
        pv_ref[0:SUBLANES, :] = pv_ref[chunk:chunk + SUBLANES, :]


def ffn_up(x, w_up, conv_w, conv_b, w_down, layer, seq_len, *, tm=1024, tn=256, chunk=128):
    t, k = x.shape
    f = w_up.shape[2] // 2
    nfb, nmt = f // tn, t // tm
    assert f % tn == 0 and seq_len % tm == 0 and t % seq_len == 0 and tm % chunk == 0
    kd, nd = w_down.shape[1:]
    slab = kd // (nfb * nmt)
    assert slab * nfb * nmt == kd and slab % (2 * SUBLANES) == 0
    cb = conv_b.reshape(conv_b.shape[0], 1, 2 * f)
    vmem = 4 * k * tn * 4 + 2 * k * tn * 2 + 2 * tm * k * 2 + 12 * tm * tn * 4 + 12 * slab * nd
    return pl.pallas_call(
        functools.partial(_ffn_up_kernel, tiles_per_seq=seq_len // tm, chunk=chunk),
        grid=(nfb, nmt),
        in_specs=[pl.BlockSpec((tm, k), lambda j, i: (i, 0)),
                  pl.BlockSpec((None, k, tn), lambda j, i: (layer, 0, j)),
                  pl.BlockSpec((None, k, tn), lambda j, i: (layer, 0, j + nfb)),
                  pl.BlockSpec((None, CONV_WIDTH, tn), lambda j, i: (layer, 0, j)),
                  pl.BlockSpec((None, CONV_WIDTH, tn), lambda j, i: (layer, 0, j + nfb)),
                  pl.BlockSpec((None, 1, tn), lambda j, i: (layer, 0, j)),
                  pl.BlockSpec((None, 1, tn), lambda j, i: (layer, 0, j + nfb)),
                  pl.BlockSpec((None, slab, nd), lambda j, i: (layer, j * nmt + i, 0))],
        out_specs=[pl.BlockSpec((tm, tn), lambda j, i: (i, j)),
                   pl.BlockSpec((slab, nd), lambda j, i: (j * nmt + i, 0))],
        out_shape=[jax.ShapeDtypeStruct((t, f), BF16), jax.ShapeDtypeStruct((kd, nd), BF16)],
        scratch_shapes=[pltpu.VMEM((k, tn), BF16), pltpu.VMEM((k, tn), BF16),
                        pltpu.VMEM((SUBLANES + chunk, tn), F32), pltpu.VMEM((SUBLANES + chunk, tn), F32)],
        compiler_params=_params(("arbitrary", "arbitrary"), vmem + (4 << 20)),
        name="ffn_up",
    )(x, w_up, w_up, conv_w, conv_w, cb, cb, w_down)


def _mla_prep_kernel(a_ref, cs_ref, gq_ref, gkv_ref, cq_ref, ckv_ref, kr_ref):
    def norm(v, g):
        return v * lax.rsqrt(jnp.mean(v * v, axis=-1, keepdims=True) + EPS) * g

    cq_ref[...] = norm(a_ref[:, 0:MLA_Q_LORA], gq_ref[...]).astype(cq_ref.dtype)
    ckv_ref[...] = norm(a_ref[:, MLA_Q_LORA:MLA_Q_LORA + MLA_KV_LORA], gkv_ref[...]).astype(ckv_ref.dtype)
    off = MLA_Q_LORA + MLA_KV_LORA
    y = a_ref[:, off:off + LANES] * cs_ref[...]
    kr_ref[...] = (y + pltpu.roll(y, MLA_ROPE, axis=1)).astype(kr_ref.dtype)


def mla_prep(a, cs, gq, gkv, tm=512):
    t, na = a.shape
    return pl.pallas_call(
        _mla_prep_kernel,
        grid=(t // tm,),
        in_specs=[pl.BlockSpec((tm, na), lambda i: (i, 0)),
                  pl.BlockSpec((tm, LANES), lambda i: (i, 0)),
                  pl.BlockSpec((1, MLA_Q_LORA), lambda i: (0, 0)),
                  pl.BlockSpec((1, MLA_KV_LORA), lambda i: (0, 0))],
        out_specs=[pl.BlockSpec((tm, MLA_Q_LORA), lambda i: (i, 0)),
                   pl.BlockSpec((tm, MLA_KV_LORA), lambda i: (i, 0)),
                   pl.BlockSpec((tm, LANES), lambda i: (i, 0))],
        out_shape=[jax.ShapeDtypeStruct((t, MLA_Q_LORA), BF16),
                   jax.ShapeDtypeStruct((t, MLA_KV_LORA), BF16),
                   jax.ShapeDtypeStruct((t, LANES), BF16)],
        compiler_params=_params(("parallel",), 6 * tm * na * 4),
        name="mla_prep",
    )(a, cs, gq.reshape(1, -1), gkv.reshape(1, -1))


def _softmax_step(s2, m, acc, v_ones):
    m_new = jnp.maximum(m, jnp.max(s2, axis=-1, keepdims=True).astype(F32))
    p = jnp.exp2(s2 - m_new.astype(BF16))
    acc = jnp.exp2(m - m_new) * acc + jnp.dot(p, v_ones, preferred_element_type=F32)
    return m_new, acc


def _softmax_init(rows):
    return jnp.full((rows, 1), -jnp.inf, F32), jnp.zeros((rows, 2 * HEAD), F32)


def _softmax_finish(acc):
    return acc[:, :HEAD] / acc[:, HEAD:]


def _mla_attn_kernel(qn_ref, qr_ref, kn_ref, v_ref, kr_ref, cs_ref, o_ref, qcat_ref, kcat_ref, vone_ref,
                     *, seq, tq, tk, scale2):
    qcat_ref[:, 0:HEAD] = qn_ref[0, 0]
    qcat_ref[:, HEAD:2 * HEAD] = (qr_ref[0, 0].astype(F32) * cs_ref[0]).astype(BF16)
    kcat_ref[:, 0:HEAD] = kn_ref[0, 0]
    kcat_ref[:, HEAD:2 * HEAD] = kr_ref[0]
    vone_ref[:, 0:HEAD] = v_ref[0, 0]
    vone_ref[:, HEAD:2 * HEAD] = jnp.ones((seq, HEAD), BF16)

    def q_block(i, _):
        r0 = pl.multiple_of(i * tq, tq)
        q = qcat_ref[pl.ds(r0, tq), :]

        def scores(c):
            return (_dot_nt(q, kcat_ref[pl.ds(pl.multiple_of(c * tk, tk), tk), :]) * scale2).astype(BF16)

        def values(c):
            return vone_ref[pl.ds(pl.multiple_of(c * tk, tk), tk), :]

        def body(c, carry):
            m, acc, s2 = carry
            s2_next = scores(c + 1)
            m, acc = _softmax_step(s2, m, acc, values(c))
            return m, acc, s2_next

        n_full = (i * tq) // tk
        m, acc, s2 = lax.fori_loop(0, n_full, body, _softmax_init(tq) + (scores(0),))
        rows = r0 + lax.broadcasted_iota(jnp.int32, (tq, tk), 0)
        cols = n_full * tk + lax.broadcasted_iota(jnp.int32, (tq, tk), 1)
        neg = jnp.full((tq, tk), -jnp.inf, BF16)
        _, acc = _softmax_step(jnp.where(cols <= rows, s2, neg), m, acc, values(n_full))
        o_ref[0, pl.ds(r0, tq), :] = _softmax_finish(acc).astype(o_ref.dtype)
        return 0

    lax.fori_loop(0, seq // tq, q_block, 0)


LOG2E = 1.4426950408889634


def mla_attention(q_hm, kv_hm, kr, cs, *, tq=1024, tk=1024):
    b, _, s, _ = q_hm.shape
    tq, tk = min(tq, s), min(tk, s)
    assert tk % tq == 0 and s % tk == 0
    blk = (1, 1, s, HEAD)
    return pl.pallas_call(
        functools.partial(_mla_attn_kernel, seq=s, tq=tq, tk=tk, scale2=(MLA_NOPE + MLA_ROPE) ** -0.5 * LOG2E),
        grid=(b, N_HEADS),
        in_specs=[pl.BlockSpec(blk, lambda bi, h: (bi, h, 0, 0)),
                  pl.BlockSpec(blk, lambda bi, h: (bi, N_HEADS + h, 0, 0)),
                  pl.BlockSpec(blk, lambda bi, h: (bi, 2 * h, 0, 0)),
                  pl.BlockSpec(blk, lambda bi, h: (bi, 2 * h + 1, 0, 0)),
                  pl.BlockSpec((1, s, HEAD), lambda bi, h: (bi, 0, 0)),
                  pl.BlockSpec((1, s, HEAD), lambda bi, h: (bi, 0, 0))],
        out_specs=pl.BlockSpec((1, s, HEAD), lambda bi, h: (bi, 0, h)),
        out_shape=jax.ShapeDtypeStruct((b, s, N_HEADS * HEAD), BF16),
        scratch_shapes=[pltpu.VMEM((s, 2 * HEAD), BF16)] * 3,
        compiler_params=_params(("parallel", "parallel"), 40 << 20),
        name="mla_attention",
    )(q_hm, q_hm, kv_hm, kv_hm, kr, cs)


SB_EXIT_LOG = -104.0


def _sb_attn_kernel(q_ref, k_ref, v_ref, u_ref, o_ref, *, seq, t, scale, heads):
    rows = lax.broadcasted_iota(jnp.int32, (t, t), 0)
    cols = lax.broadcasted_iota(jnp.int32, (t, t), 1)
    past = cols < rows

    def tile(h, q, j, c, diag):
        k0 = pl.multiple_of(j * t, t)
        z = _dot_nt(q, k_ref[0, h, pl.ds(k0, t), :]) * scale
        log_beta = jnp.minimum(z, 0.0) - jnp.log(1.0 + jnp.exp(-jnp.abs(z)))
        log_keep = log_beta - z
        if diag:
            log_keep = jnp.where(past, log_keep, 0.0)
        hi = log_keep.astype(BF16)
        lo = (log_keep - hi.astype(F32)).astype(BF16)
        tri = u_ref[...]
        later = jnp.dot(hi, tri, preferred_element_type=F32) + jnp.dot(lo, tri, preferred_element_type=F32)
        a = jnp.exp(log_beta + later + c)
        if diag:
            a = jnp.where(past, a, 0.0)
        pv = jnp.dot(a.astype(BF16), v_ref[0, h, pl.ds(k0, t), :], preferred_element_type=F32)
        return pv, jnp.sum(log_keep, axis=-1, keepdims=True)

    def q_block(i, _):
        r0 = pl.multiple_of(i * t, t)
        qs = [q_ref[0, h, pl.ds(r0, t), :] for h in range(heads)]

        def step(j, cs, accs, diag):
            outs = [tile(h, qs[h], j, cs[h], diag) for h in range(heads)]
            return (tuple(c + o[1] for c, o in zip(cs, outs)), tuple(a + o[0] for a, o in zip(accs, outs)))

        def any_live(cs):
            top = cs[0]
            for c in cs[1:]:
                top = jnp.maximum(top, c)
            return (jnp.max(top) > SB_EXIT_LOG).astype(jnp.int32)

        zeros = lambda w: tuple(jnp.zeros((t, w), F32) for _ in range(heads))
        cs, accs = step(i, zeros(1), zeros(HEAD), True)

        def cond(carry):
            return jnp.logical_and(carry[0] >= 0, carry[3] > 0)

        def body(carry):
            j, cs, accs, _ = carry
            cs, accs = step(j, cs, accs, False)
            return j - 1, cs, accs, any_live(cs)

        _, _, accs, _ = lax.while_loop(cond, body, (i - 1, cs, accs, any_live(cs)))
        for h in range(heads):
            o_ref[0, pl.ds(r0, t), h * HEAD:(h + 1) * HEAD] = accs[h].astype(o_ref.dtype)
        return 0

    lax.fori_loop(0, seq // t, q_block, 0)


def sb_attention(qkv_hm, *, t=256, heads=4):
    b, _, s, _ = qkv_hm.shape
    blk = (1, heads, s, HEAD)
    nhb = N_HEADS // heads
    tri = (jnp.arange(t)[:, None] > jnp.arange(t)[None, :]).astype(BF16)
    return pl.pallas_call(
        functools.partial(_sb_attn_kernel, seq=s, t=t, scale=HEAD ** -0.5, heads=heads),
        grid=(b, nhb),
        in_specs=[pl.BlockSpec(blk, lambda bi, h: (bi, h, 0, 0)),
                  pl.BlockSpec(blk, lambda bi, h: (bi, nhb + h, 0, 0)),
                  pl.BlockSpec(blk, lambda bi, h: (bi, 2 * nhb + h, 0, 0)),
                  pl.BlockSpec((t, t), lambda bi, h: (0, 0))],
        out_specs=pl.BlockSpec((1, s, heads * HEAD), lambda bi, h: (bi, 0, h)),
        out_shape=jax.ShapeDtypeStruct((b, s, N_HEADS * HEAD), BF16),
        compiler_params=_params(("parallel", "parallel"), 40 << 20),
        name="sb_attention",
    )(qkv_hm, qkv_hm, qkv_hm, tri)


DSA_TQ = 256
DSA_TK = 1024


def _dsa_index_kernel(iq_ref, tail_ref, iw_ref, g_ref, bias_ref, ikn_ref, key_ref,
                      *, n_chunks, top_k, w_scale, heads_per_dot):
    i = pl.program_id(1)

    @pl.when(i == 0)
    def _():
        ik = tail_ref[0]
        y = ik * lax.rsqrt(jnp.mean(ik * ik, axis=-1, keepdims=True) + EPS)
        ikn_ref[...] = (y * g_ref[...]).astype(BF16)

    n_live = (i * DSA_TQ) // DSA_TK + 1
    w = iw_ref[0][:, 0:N_HEADS] * w_scale
    rows = i * DSA_TQ + lax.broadcasted_iota(jnp.int32, (DSA_TQ, DSA_TK), 0)
    cols0 = lax.broadcasted_iota(jnp.int32, (DSA_TQ, DSA_TK), 1)

    def score_chunk(c, _):
        k0 = pl.multiple_of(c * DSA_TK, DSA_TK)
        ik = ikn_ref[pl.ds(k0, DSA_TK), :]
        sc = jnp.zeros((DSA_TQ, DSA_TK), F32)
        for hg in range(N_HEADS // heads_per_dot):
            q = iq_ref[0, hg * heads_per_dot:(hg + 1) * heads_per_dot].reshape(heads_per_dot * DSA_TQ, HEAD)
            d = _dot_nt(q, ik)
            for hh in range(heads_per_dot):
                h = hg * heads_per_dot + hh
                sc = sc + w[:, h:h + 1] * jnp.maximum(d[hh * DSA_TQ:(hh + 1) * DSA_TQ], 0.0)
        bits = pltpu.bitcast(sc, jnp.int32)
        key = jnp.where(bits < 0, bits ^ jnp.int32(0x7FFFFFFF), bits)
        key_ref[c] = jnp.where(k0 + cols0 <= rows, key, jnp.int32(INT_MIN))
        return 0

    lax.fori_loop(0, n_live, score_chunk, 0)

    def fill_dead(c, _):
        key_ref[c] = jnp.full((DSA_TQ, DSA_TK), INT_MIN, jnp.int32)
        return 0

    def bisect(n_counted):
        lax.fori_loop(n_live, n_counted, fill_dead, 0)

        def bit_step(n, thr):
            cand = thr + lax.shift_left(jnp.int32(1), jnp.int32(31) - n)
            cnt = jnp.zeros((DSA_TQ, DSA_TK), jnp.int32)
            for c in range(n_counted):
                cnt = cnt + (key_ref[c] >= cand).astype(jnp.int32)
            return jnp.where(jnp.sum(cnt, axis=-1, keepdims=True) >= top_k, cand, thr)

        return lax.fori_loop(0, 32, bit_step, jnp.full((DSA_TQ, 1), INT_MIN, jnp.int32))

    half = n_chunks // 2
    if half >= 1:
        thr = lax.cond(n_live <= half, lambda: bisect(half), lambda: bisect(n_chunks))
    else:
        thr = bisect(n_chunks)
    thr = jnp.maximum(thr, jnp.int32(INT_MIN + 1))

    def write_live(c, _):
        bias_ref[0, 0, c] = jnp.where(key_ref[c] >= thr, 0.0, NEG_BIG).astype(bias_ref.dtype)
        return 0

    def write_dead(c, _):
        bias_ref[0, 0, c] = jnp.full((DSA_TQ, DSA_TK), NEG_BIG, bias_ref.dtype)
        return 0

    lax.fori_loop(0, n_live, write_live, 0)
    lax.fori_loop(n_live, n_chunks, write_dead, 0)


def dsa_index(iq_hm, tail, g_ik, top_k):
    b, _, s, _ = iq_hm.shape
    nq, nc = s // DSA_TQ, s // DSA_TK
    w_scale = N_HEADS ** -0.5 * HEAD ** -0.5
    return pl.pallas_call(
        functools.partial(_dsa_index_kernel, n_chunks=nc, top_k=top_k, w_scale=w_scale, heads_per_dot=8),
        grid=(b, nq),
        in_specs=[pl.BlockSpec((1, N_HEADS, DSA_TQ, HEAD), lambda bi, i: (bi, 0, i, 0)),
                  pl.BlockSpec((1, s, HEAD), lambda bi, i: (bi, 0, 0)),
                  pl.BlockSpec((1, DSA_TQ, HEAD), lambda bi, i: (bi, i, 1)),
                  pl.BlockSpec((1, HEAD), lambda bi, i: (0, 0))],
        out_specs=pl.BlockSpec((1, 1, nc, DSA_TQ, DSA_TK), lambda bi, i: (bi, i, 0, 0, 0)),
        out_shape=jax.ShapeDtypeStruct((b, nq, nc, DSA_TQ, DSA_TK), BF16),
        scratch_shapes=[pltpu.VMEM((s, HEAD), BF16), pltpu.VMEM((nc, DSA_TQ, DSA_TK), jnp.int32)],
        compiler_params=_params(("arbitrary", "arbitrary"), 40 << 20),
        name="dsa_index",
    )(iq_hm, tail, tail, g_ik.reshape(1, HEAD))


def _dsa_attn_kernel(q_ref, k_ref, v_ref, bias_ref, pq_ref, pk_ref, slope_ref, o_ref, *, scale2):
    i = pl.program_id(2)
    rq = DSA_GROUP * DSA_TQ
    q = q_ref[0].reshape(rq, HEAD)
    slope = slope_ref[0].reshape(DSA_GROUP, DSA_TQ, 1)
    pq = pq_ref[0]

    def step(c, carry):
        m, l, acc = carry
        k0 = pl.multiple_of(c * DSA_TK, DSA_TK)
        raw = _dot_nt(q, k_ref[0, 0, pl.ds(k0, DSA_TK), :])
        g = bias_ref[0, 0, c].astype(F32) - jnp.abs(pq - pk_ref[0, c]).astype(F32)
        u = (raw.reshape(DSA_GROUP, DSA_TQ, DSA_TK) + slope * g[None]).reshape(rq, DSA_TK)
        m_new = jnp.maximum(m, jnp.max(u, axis=-1, keepdims=True))
        alpha = jnp.exp2((m - m_new) * scale2)
        p = jnp.exp2((u - m_new) * scale2)
        l = alpha * l + jnp.sum(p, axis=-1, keepdims=True)
        acc = alpha * acc + jnp.dot(p.astype(BF16), v_ref[0, 0, pl.ds(k0, DSA_TK), :], preferred_element_type=F32)
        return m_new, l, acc

    init = (jnp.full((rq, 1), -jnp.inf, F32), jnp.zeros((rq, 1), F32), jnp.zeros((rq, HEAD), F32))
    _, l, acc = lax.fori_loop(0, (i * DSA_TQ) // DSA_TK + 1, step, init)
    o = acc / l
    for r in range(DSA_GROUP):
        o_ref[0, :, r * HEAD:(r + 1) * HEAD] = o[r * DSA_TQ:(r + 1) * DSA_TQ].astype(o_ref.dtype)


def dsa_attention(qkv_hm, bias, positions, slopes):
    b, _, s, _ = qkv_hm.shape
    nq, nc = s // DSA_TQ, s // DSA_TK
    kv = (1, 1, s, HEAD)
    scale = HEAD ** -0.5
    slope_rows = jnp.repeat((slopes / scale).reshape(DSA_KV_HEADS, DSA_GROUP), DSA_TQ, axis=1)[..., None]
    return pl.pallas_call(
        functools.partial(_dsa_attn_kernel, scale2=scale * LOG2E),
        grid=(b, DSA_KV_HEADS, nq),
        in_specs=[pl.BlockSpec((1, DSA_GROUP, DSA_TQ, HEAD), lambda bi, g, i: (bi, g, i, 0)),
                  pl.BlockSpec(kv, lambda bi, g, i: (bi, N_HEADS + g, 0, 0)),
                  pl.BlockSpec(kv, lambda bi, g, i: (bi, N_HEADS + DSA_KV_HEADS + g, 0, 0)),
                  pl.BlockSpec((1, 1, nc, DSA_TQ, DSA_TK), lambda bi, g, i: (bi, i, 0, 0, 0)),
                  pl.BlockSpec((1, DSA_TQ, 1), lambda bi, g, i: (bi, i, 0)),
                  pl.BlockSpec((1, nc, 1, DSA_TK), lambda bi, g, i: (bi, 0, 0, 0)),
                  pl.BlockSpec((1, DSA_GROUP * DSA_TQ, 1), lambda bi, g, i: (g, 0, 0))],
        out_specs=pl.BlockSpec((1, DSA_TQ, DSA_GROUP * HEAD), lambda bi, g, i: (bi, i, g)),
        out_shape=jax.ShapeDtypeStruct((b, s, N_HEADS * HEAD), BF16),
        compiler_params=_params(("parallel", "parallel", "arbitrary"), 32 << 20),
        name="dsa_attention",
    )(qkv_hm, qkv_hm, qkv_hm, bias, positions.reshape(b, s, 1), positions.reshape(b, nc, 1, DSA_TK),
      slope_rows)


def _rotate_half_cols(r):
    half = r.shape[-1] // 2
    return jnp.concatenate([-r[..., half:], r[..., :half]], axis=-1)


def _mla_layer(n, h, cs, b, j, w_a, q_norm, kv_norm, w_uq, w_ukv, w_o):
    t, d = n.shape
    lat = MLA_Q_LORA + MLA_KV_LORA
    kr_w = w_a[j, :, lat:lat + MLA_ROPE]
    w_a2 = jnp.concatenate([w_a[j, :, :lat], kr_w, _rotate_half_cols(kr_w), jnp.zeros((d, LANES), F32)], axis=1)
    a = proj(n, w_a2, w_a2.shape[1], tm=1024, tn=256, out_dtype=F32)
    cq, ckv, kr = mla_prep(a, cs, q_norm[j], kv_norm[j])
    w3 = w_uq[j].reshape(MLA_Q_LORA, N_HEADS, MLA_NOPE + MLA_ROPE)
    rope_w = w3[:, :, MLA_NOPE:]
    w_uq2 = jnp.concatenate([w3[:, :, :MLA_NOPE].reshape(MLA_Q_LORA, -1),
                             jnp.concatenate([rope_w, _rotate_half_cols(rope_w)], -1).reshape(MLA_Q_LORA, -1)], 1)
    q_hm = proj(cq, w_uq2, w_uq2.shape[1], tm=1024, tn=512, out_dtype=BF16, heads_batch=b)
    kv_hm = proj(ckv, w_ukv, w_ukv.shape[2], layer=j, tm=1024, tn=512, out_dtype=BF16, heads_batch=b)
    s = t // b
    o = mla_attention(q_hm, kv_hm, kr.reshape(b, s, LANES), cs.reshape(b, s, LANES))
    return proj(o.reshape(t, d), w_o, d, layer=j, tm=1024, tn=512, out_dtype=F32, res=h)


def _sb_layer(n, h, b, j, w_qkv, w_o):
    t, d = n.shape
    qkv_hm = proj(n, w_qkv, w_qkv.shape[2], layer=j, tm=1024, tn=512, out_dtype=BF16, heads_batch=b)
    o = sb_attention(qkv_hm)
    return proj(o.reshape(t, d), w_o, d, layer=j, tm=1024, tn=512, out_dtype=F32, res=h)


def _dsa_layer(n, h, positions, b, j, w_in, idx_k_norm, w_o):
    t, d = n.shape
    s = t // b
    n_qkv = (N_HEADS + 2 * DSA_KV_HEADS) * HEAD
    n_iq = N_HEADS * HEAD
    qkv_hm = proj(n, w_in, n_qkv, layer=j, tm=1024, tn=512, out_dtype=BF16, heads_batch=b)
    iq_hm = proj(n, w_in, n_iq, layer=j, col_off=n_qkv, tm=1024, tn=512, out_dtype=BF16, heads_batch=b)
    w_tail = jnp.pad(w_in[j, :, n_qkv + n_iq:], ((0, 0), (0, 2 * LANES - (HEAD + N_HEADS))))
    tail = proj(n, w_tail, 2 * LANES, tm=1024, tn=256, out_dtype=F32)
    bias = dsa_index(iq_hm, tail.reshape(b, s, 2 * LANES), idx_k_norm[j], min(DSA_TOPK_MAX, s // 4))
    slopes = jnp.exp2(-8.0 * jnp.arange(1, N_HEADS + 1, dtype=F32) / N_HEADS)
    o = dsa_attention(qkv_hm, bias, positions, slopes)
    return proj(o.reshape(t, d), w_o, d, layer=j, tm=1024, tn=512, out_dtype=F32, res=h)


def _rope_table(positions):
    inv = ROPE_THETA ** (-jnp.arange(0, MLA_ROPE, 2, dtype=F32) / MLA_ROPE)
    ang = positions.astype(F32)[..., None] * inv
    cos, sin = jnp.cos(ang), jnp.sin(ang)
    return jnp.concatenate([cos, cos, sin, sin], axis=-1)


@jax.jit
def kernel(x, positions, attn_norm, ffn_norm, final_norm, ffn_w_up, ffn_conv_w, ffn_conv_b, ffn_w_down,
           mla_w_a, mla_q_norm, mla_kv_norm, mla_w_uq, mla_w_ukv, mla_w_o, sb_w_qkv, sb_w_o,
           dsa_w_in, dsa_idx_k_norm, dsa_w_o):
    b, s, d = x.shape
    t = b * s
    h = x.reshape(t, d)
    cs = _rope_table(positions).reshape(t, LANES)
    for i in range(DEPTH):
        kind, j = i % N_MIXERS, i // N_MIXERS
        n = rmsnorm(h, attn_norm[i], BF16)
        if kind == 0:
            h = _mla_layer(n, h, cs, b, j, mla_w_a, mla_q_norm, mla_kv_norm, mla_w_uq, mla_w_ukv, mla_w_o)
        elif kind == 1:
            h = _sb_layer(n, h, b, j, sb_w_qkv, sb_w_o)
        else:
            h = _dsa_layer(n, h, positions, b, j, dsa_w_in, dsa_idx_k_norm, dsa_w_o)
        n2 = rmsnorm(h, ffn_norm[i], BF16)
        act, w_down = ffn_up(n2, ffn_w_up, ffn_conv_w, ffn_conv_b, ffn_w_down, i, s)
        h = down_proj(act, w_down, h)
    return rmsnorm(h, final_norm, F32).reshape(b, s, d)
```
